```python
import math
import jax
import jax.numpy as jnp
from jax import lax
import numpy as np

D_MODEL = 2048
BATCH = 2
SEQ = 4096
DEPTH = 2

FNET_GROUPS = 4
FNET_GROUP_DIM = 128
FNET_WIDTH = FNET_GROUPS * FNET_GROUP_DIM
HGRN_HEADS = 6
HGRN_EXPAND = 128
HGRN_HEAD_DV = 128
HGRN_DK = HGRN_HEADS * HGRN_EXPAND
HGRN_DV = HGRN_HEADS * HGRN_HEAD_DV
HGRN_CHUNK = 64
HGRN_NORM_EPS = 1e-6
LB_FLOOR = 1e-30
ATTN_GROUPS = ((128, 1), (512, 4), (2048, 16))
ATTN_HEADS_PER_GROUP = 4
ATTN_HEAD_DIM = 64
ATTN_HEADS = ATTN_HEADS_PER_GROUP * len(ATTN_GROUPS)
ATTN_WIDTH = ATTN_HEADS * ATTN_HEAD_DIM
ATTN_OUT_WIDTH = ATTN_HEADS_PER_GROUP * ATTN_HEAD_DIM
ATTN_BLOCK = 128
ROPE_THETA = 500000.0
ROPE_DIM = ATTN_HEAD_DIM // 4
NEG_BIG = -1e30
N_BRANCHES = 3
IN_WIDTHS = (FNET_WIDTH, HGRN_DK, HGRN_DK, HGRN_DK, HGRN_DV, HGRN_DV,
             ATTN_WIDTH, ATTN_WIDTH, ATTN_WIDTH, N_BRANCHES * D_MODEL)
W_IN = sum(IN_WIDTHS)
PEER_HEADS = 8
PEER_N_KEYS = 128
PEER_N_EXPERTS = PEER_N_KEYS * PEER_N_KEYS
PEER_QUERY_DIM = 256
PEER_HALF = PEER_QUERY_DIM // 2
PEER_TOPK = 16
PEER_TOKEN_BLOCK = 128
DN_ALPHA = (2.0 * DEPTH) ** 0.25
DN_BETA = (8.0 * DEPTH) ** -0.25
LN_EPS = 1e-5

kernel_name = "hybrid_fnet_hgrn2_dilated_peer_encoder"


def layer_norm(x, g, b):
    xf = x.astype(jnp.float32)
    mu = jnp.mean(xf, axis=-1, keepdims=True)
    var = jnp.mean(jnp.square(xf - mu), axis=-1, keepdims=True)
    y = (xf - mu) * lax.rsqrt(var + LN_EPS) * g.astype(jnp.float32) + b.astype(jnp.float32)
    return y.astype(x.dtype)


def split_cols(h, widths):
    out, start = [], 0
    for w in widths:
        out.append(h[..., start:start + w])
        start += w
    return out


def fourier_branch(u):
    B, S, _ = u.shape
    ug = u.reshape(B, S, FNET_GROUPS, FNET_GROUP_DIM).astype(jnp.float32)
    y = jnp.fft.fft2(ug, axes=(1, 3), norm="ortho").real
    return y.reshape(B, S, FNET_WIDTH).astype(u.dtype)


def gla_chunk_scan(q, k, v, logf):
    B, H, S, dk = q.shape
    dv = v.shape[-1]
    C = HGRN_CHUNK
    n = S // C

    def chunks(t):
        return jnp.moveaxis(t.reshape(B, H, n, C, t.shape[-1]), 2, 0)

    lower = jnp.tril(jnp.ones((C, C), dtype=bool))[:, :, None]

    def step(state, inp):
        qc, kc, vc, gc = inp
        b = jnp.cumsum(gc, axis=2)
        o = jnp.einsum('bhtk,bhkv->bhtv', qc * jnp.exp(b), state)
        diff = b[:, :, :, None, :] - b[:, :, None, :, :]
        decay = jnp.where(lower, jnp.exp(jnp.where(lower, diff, 0.0)), 0.0)
        a = jnp.einsum('bhtk,bhsk,bhtsk->bhts', qc, kc, decay)
        o = o + jnp.einsum('bhts,bhsv->bhtv', a, vc)
        b_last = b[:, :, -1:, :]
        state = (jnp.exp(b_last[:, :, 0, :])[..., None] * state
                 + jnp.einsum('bhsk,bhsv->bhkv', kc * jnp.exp(b_last - b), vc))
        return state, o

    s0 = jnp.zeros((B, H, dk, dv), jnp.float32)
    _, o = lax.scan(step, s0, (chunks(q), chunks(k), chunks(v), chunks(logf)))
    return jnp.moveaxis(o, 0, 2).reshape(B, H, S, dv)


def hgrn2_branch(f_fwd_pre, f_bwd_pre, q, i, g, lb_fwd, lb_bwd, norm_g):
    dtype = q.dtype
    B, S, _ = q.shape

    def heads(t, d):
        return t.astype(jnp.float32).reshape(B, S, HGRN_HEADS, d).transpose(0, 2, 1, 3)

    qh = heads(q, HGRN_EXPAND)
    vh = heads(i, HGRN_HEAD_DV)

    def forget(z_pre, lb):
        z = heads(z_pre, HGRN_EXPAND)
        lbh = lb.astype(jnp.float32).reshape(HGRN_HEADS, 1, HGRN_EXPAND)
        log_lb = jnp.log(jnp.maximum(lbh, LB_FLOOR))
        logf = jnp.logaddexp(log_lb, jnp.log1p(-lbh) + jax.nn.log_sigmoid(z))
        key = (1.0 - lbh) * jax.nn.sigmoid(-z)
        return key, logf

    k_f, lf_f = forget(f_fwd_pre, lb_fwd)
    k_b, lf_b = forget(f_bwd_pre, lb_bwd)
    o_f = gla_chunk_scan(qh, k_f, vh, lf_f)
    flip = lambda t: jnp.flip(t, axis=2)
    o_b = flip(gla_chunk_scan(flip(qh), flip(k_b), flip(vh), flip(lf_b)))
    o = o_f + o_b
    o = o * lax.rsqrt(jnp.mean(jnp.square(o), axis=-1, keepdims=True) + HGRN_NORM_EPS)
    o = o.transpose(0, 2, 1, 3).reshape(B, S, HGRN_DV) * norm_g.astype(jnp.float32)
    return (o * jax.nn.silu(g.astype(jnp.float32))).astype(dtype)


def partial_rope(t, cos, sin):
    half = ROPE_DIM // 2
    c = cos[None, :, None, :].astype(t.dtype)
    s = sin[None, :, None, :].astype(t.dtype)
    t1 = t[..., :half]
    t2 = t[..., half:ROPE_DIM]
    return jnp.concatenate([t1 * c - t2 * s, t2 * c + t1 * s, t[..., ROPE_DIM:]], axis=-1)


def band_attend(q, k, v, radius, block):
    N, L, dh = q.shape
    nb = -(-L // block)
    Lp = nb * block
    width = block + 2 * radius
    qp = jnp.pad(q, ((0, 0), (0, Lp - L), (0, 0))).reshape(N, nb, block, dh)
    pad_kv = ((0, 0), (radius, Lp - L + radius), (0, 0))
    kp = jnp.pad(k, pad_kv)
    vp = jnp.pad(v, pad_kv)
    idx = (jnp.arange(nb) * block)[:, None] + jnp.arange(width)[None, :]
    kb = kp[:, idx]
    vb = vp[:, idx]
    s = jnp.einsum('nbqd,nbkd->nbqk', qp, kb).astype(jnp.float32) * (1.0 / math.sqrt(dh))
    qpos = (jnp.arange(nb) * block)[:, None] + jnp.arange(block)[None, :]
    kpos = idx - radius
    rel = kpos[:, None, :] - qpos[:, :, None]
    valid = (jnp.abs(rel) <= radius) & (kpos[:, None, :] >= 0) & (kpos[:, None, :] < L)
    s = jnp.where(valid[None], s, NEG_BIG)
    lse = jax.nn.logsumexp(s, axis=-1)
    p = jnp.exp(s - lse[..., None])
    o = jnp.einsum('nbqk,nbkd->nbqd', p.astype(v.dtype), vb)
    return o.reshape(N, Lp, dh)[:, :L], lse.reshape(N, Lp)[:, :L]


def dilated_attention_branch(q, k, v, cos, sin):
    B, S, _ = q.shape
    G = ATTN_HEADS_PER_GROUP
    dh = ATTN_HEAD_DIM
    qh = partial_rope(q.reshape(B, S, ATTN_HEADS, dh), cos, sin)
    kh = partial_rope(k.reshape(B, S, ATTN_HEADS, dh), cos, sin)
    vh = v.reshape(B, S, ATTN_HEADS, dh)
    outs, lses = [], []
    for gi, (window, dil) in enumerate(ATTN_GROUPS):
        radius = window // (2 * dil)
        L = S // dil

        def to_sub(t):
            t = t[:, :, gi * G:(gi + 1) * G]
            return t.reshape(B, L, dil, G, dh).transpose(0, 2, 3, 1, 4).reshape(B * dil * G, L, dh)

        o, lse = band_attend(to_sub(qh), to_sub(kh), to_sub(vh), radius, min(ATTN_BLOCK, L))
        outs.append(o.reshape(B, dil, G, L, dh).transpose(0, 3, 1, 2, 4).reshape(B, S, G, dh))
        lses.append(lse.reshape(B, dil, G, L).transpose(0, 3, 1, 2).reshape(B, S, G))
    w = jax.nn.softmax(jnp.stack(lses, axis=0), axis=0)
    out = jnp.sum(w[..., None] * jnp.stack(outs, axis=0).astype(jnp.float32), axis=0)
    return out.reshape(B, S, ATTN_OUT_WIDTH).astype(q.dtype)


def hybrid_mixer(x, w_in, w_br_f, w_br_h, w_br_a, w_out, lb_fwd, lb_bwd, hgrn_norm_g, cos, sin):
    B, S, D = x.shape
    h = x @ w_in
    (u_f, f_fwd, f_bwd, q_h, i_h, g_h, q_a, k_a, v_a, gate_pre) = split_cols(h, IN_WIDTHS)
    y_f = fourier_branch(u_f)
    y_h = hgrn2_branch(f_fwd, f_bwd, q_h, i_h, g_h, lb_fwd, lb_bwd, hgrn_norm_g)
    y_a = dilated_attention_branch(q_a, k_a, v_a, cos, sin)
    gates = jax.nn.sigmoid(gate_pre.reshape(B, S, N_BRANCHES, D))
    merged = (gates[:, :, 0] * (y_f @ w_br_f)
              + gates[:, :, 1] * (y_h @ w_br_h)
              + gates[:, :, 2] * (y_a @ w_br_a))
    return merged @ w_out


def peer_ffn(x, w_query, sub_keys, u, v):
    B, S, D = x.shape
    T = B * S
    xt = x.reshape(T, D)
    qh = (xt @ w_query).reshape(T, PEER_HEADS, 2, PEER_HALF)
    s = jnp.einsum('thpd,pkd->thpk', qh, sub_keys).astype(jnp.float32)
    top_s, top_i = lax.top_k(s, PEER_TOPK)
    cand = top_s[:, :, 0, :, None] + top_s[:, :, 1, None, :]
    best_s, best_c = lax.top_k(cand.reshape(T, PEER_HEADS, PEER_TOPK * PEER_TOPK), PEER_TOPK)
    i1 = jnp.take_along_axis(top_i[:, :, 0], best_c // PEER_TOPK, axis=-1)
    i2 = jnp.take_along_axis(top_i[:, :, 1], best_c % PEER_TOPK, axis=-1)
    experts = (i1 * PEER_N_KEYS + i2).reshape(T, PEER_HEADS * PEER_TOPK)
    gate = jax.nn.softmax(best_s, axis=-1).reshape(T, PEER_HEADS * PEER_TOPK).astype(x.dtype)
    nblk = T // PEER_TOKEN_BLOCK

    def token_block(args):
        xb, eb, gb = args
        hidden = jax.nn.gelu(jnp.einsum('td,tkd->tk', xb, jnp.take(u, eb, axis=0)), approximate=False)
        return jnp.einsum('tk,tkd->td', gb * hidden, jnp.take(v, eb, axis=0))

    y = lax.map(token_block, (xt.reshape(nblk, PEER_TOKEN_BLOCK, D),
                              experts.reshape(nblk, PEER_TOKEN_BLOCK, -1),
                              gate.reshape(nblk, PEER_TOKEN_BLOCK, -1)))
    return y.reshape(B, S, D)


def setup_inputs(seed: int = 0) -> dict:
    key = jax.random.key(seed)
    ks = jax.random.split(key, 18)
    f32 = jnp.float32

    def nrm(k, shape, scale):
        return jax.random.normal(k, shape, f32) * scale

    D = D_MODEL
    return {
        "x": nrm(ks[0], (BATCH, SEQ, D), 1.0),
        "emb_ln_g": 1.0 + nrm(ks[1], (D,), 0.02),
        "emb_ln_b": nrm(ks[2], (D,), 0.02),
        "w_in": nrm(ks[3], (DEPTH, D, W_IN), D ** -0.5),
        "w_br_fourier": nrm(ks[4], (DEPTH, FNET_WIDTH, D), DN_BETA * FNET_WIDTH ** -0.5),
        "w_br_hgrn": nrm(ks[5], (DEPTH, HGRN_DV, D), DN_BETA * HGRN_DV ** -0.5),
        "w_br_attn": nrm(ks[6], (DEPTH, ATTN_OUT_WIDTH, D), DN_BETA * ATTN_OUT_WIDTH ** -0.5),
        "w_out": nrm(ks[7], (DEPTH, D, D), DN_BETA * D ** -0.5),
        "hgrn_lb_logits": nrm(ks[8], (2, DEPTH, HGRN_DK), 0.5),
        "hgrn_norm_g": 1.0 + nrm(ks[9], (DEPTH, HGRN_DV), 0.02),
        "ln_mix_g": 1.0 + nrm(ks[10], (DEPTH, D), 0.02),
        "ln_mix_b": nrm(ks[11], (DEPTH, D), 0.02),
        "peer_w_query": nrm(ks[12], (DEPTH, D, PEER_HEADS * PEER_QUERY_DIM), D ** -0.5),
        "peer_sub_keys": nrm(ks[13], (DEPTH, 2, PEER_N_KEYS, PEER_HALF), PEER_HALF ** -0.5),
        "peer_u": nrm(ks[14], (DEPTH, PEER_N_EXPERTS, D), D ** -0.5),
        "peer_v": nrm(ks[15], (DEPTH, PEER_N_EXPERTS, D), DN_BETA),
        "ln_ffn_g": 1.0 + nrm(ks[16], (DEPTH, D), 0.02),
        "ln_ffn_b": nrm(ks[17], (DEPTH, D), 0.02),
    }


def reference(x, emb_ln_g, emb_ln_b, w_in, w_br_fourier, w_br_hgrn, w_br_attn, w_out,
              hgrn_lb_logits, hgrn_norm_g, ln_mix_g, ln_mix_b, peer_w_query, peer_sub_keys,
              peer_u, peer_v, ln_ffn_g, ln_ffn_b):
    S = x.shape[1]
    pos = jnp.arange(S, dtype=jnp.float32)
    inv_freq = ROPE_THETA ** (-jnp.arange(0, ROPE_DIM, 2, dtype=jnp.float32) / ROPE_DIM)
    ang = pos[:, None] * inv_freq[None, :]
    cos, sin = jnp.cos(ang), jnp.sin(ang)
    lb_p = jax.nn.softmax(hgrn_lb_logits.astype(jnp.float32), axis=1)
    lb = jnp.cumsum(lb_p, axis=1) - lb_p[:, :1]
    x = layer_norm(x, emb_ln_g, emb_ln_b)
    for l in range(DEPTH):
        mix = hybrid_mixer(x, w_in[l], w_br_fourier[l], w_br_hgrn[l], w_br_attn[l], w_out[l],
                           lb[0, l], lb[1, l], hgrn_norm_g[l], cos, sin)
        x = layer_norm(DN_ALPHA * x + mix, ln_mix_g[l], ln_mix_b[l])
        ffn = peer_ffn(x, peer_w_query[l], peer_sub_keys[l], peer_u[l], peer_v[l])
        x = layer_norm(DN_ALPHA * x + ffn, ln_ffn_g[l], ln_ffn_b[l])
    return x
```

```python
import functools
import math

import jax
import jax.numpy as jnp
from jax import lax
from jax.experimental import pallas as pl
from jax.experimental.pallas import tpu as pltpu

F32 = jnp.float32
BF16 = jnp.bfloat16

V7X_VMEM_BYTES = 64 * 1024 * 1024
LANES = 128
MIB = 1024 * 1024

FNET_GROUPS = 4
FNET_GROUP_DIM = 128
FNET_WIDTH = FNET_GROUPS * FNET_GROUP_DIM
HGRN_HEADS = 6
HGRN_DK = HGRN_HEADS * 128
HGRN_CHUNK = 64
HGRN_NORM_EPS = 1e-6
LB_FLOOR = 1e-30
ATTN_GROUPS = ((128, 1), (512, 4), (2048, 16))
ATTN_G = 4
ATTN_DH = 64
ATTN_HEADS = ATTN_G * len(ATTN_GROUPS)
ATTN_WIDTH = ATTN_HEADS * ATTN_DH
ATTN_BLOCK = 128
ROPE_THETA = 500000.0
ROPE_DIM = ATTN_DH // 4
NEG_BIG = -1e30
N_BRANCHES = 3
PEER_HEADS = 8
PEER_N_KEYS = 128
PEER_TOPK = 16
LN_EPS = 1e-5

OFF_U = 0
OFF_FF = OFF_U + FNET_WIDTH
OFF_FB = OFF_FF + HGRN_DK
OFF_Q = OFF_FB + HGRN_DK
OFF_I = OFF_Q + HGRN_DK
OFF_G = OFF_I + HGRN_DK
OFF_QA = OFF_G + HGRN_DK
OFF_KA = OFF_QA + ATTN_WIDTH
OFF_VA = OFF_KA + ATTN_WIDTH
OFF_GATE = OFF_VA + ATTN_WIDTH

NT_DIMS = (((1,), (1,)), ((), ()))
TN_DIMS = (((0,), (0,)), ((), ()))


def _cparams(semantics, block_bytes, temp_bytes=0):
    need = 2 * block_bytes + temp_bytes + 4 * MIB
    limit = int(min(V7X_VMEM_BYTES - 8 * MIB, max(need, 16 * MIB)))
    return pltpu.CompilerParams(dimension_semantics=semantics, vmem_limit_bytes=limit)


def _ln_rows(v, g, b):
    mu = jnp.mean(v, axis=-1, keepdims=True)
    d = v - mu
    var = jnp.mean(d * d, axis=-1, keepdims=True)
    return d * lax.rsqrt(var + LN_EPS) * g + b


def _ln_kernel(x_ref, g_ref, b_ref, of_ref, ob_ref):
    y = _ln_rows(x_ref[...], g_ref[...], b_ref[...])
    of_ref[...] = y
    ob_ref[...] = y.astype(BF16)


def _ln_res_kernel(x_ref, y_ref, g_ref, b_ref, of_ref, ob_ref, *, alpha):
    y = _ln_rows(alpha * x_ref[...] + y_ref[...], g_ref[...], b_ref[...])
    of_ref[...] = y
    ob_ref[...] = y.astype(BF16)


def _layer_norm(x, g, b, res=None, alpha=1.0, tm=256):
    T, D = x.shape
    row = pl.BlockSpec((tm, D), lambda i: (i, 0))
    vec = pl.BlockSpec((1, D), lambda i: (0, 0))
    out_shape = (jax.ShapeDtypeStruct((T, D), F32), jax.ShapeDtypeStruct((T, D), BF16))
    g2, b2 = g.reshape(1, D).astype(F32), b.reshape(1, D).astype(F32)
    blk = tm * D * 4
    if res is None:
        return pl.pallas_call(
            _ln_kernel, grid=(T // tm,), in_specs=[row, vec, vec], out_specs=(row, row),
            out_shape=out_shape, compiler_params=_cparams(("parallel",), 3 * blk, 2 * blk),
            name="ln")(x, g2, b2)
    return pl.pallas_call(
        functools.partial(_ln_res_kernel, alpha=alpha), grid=(T // tm,),
        in_specs=[row, row, vec, vec], out_specs=(row, row), out_shape=out_shape,
        compiler_params=_cparams(("parallel",), 4 * blk, 2 * blk), name="ln_res")(x, res, g2, b2)


def _mm_kernel(a_ref, b_ref, o_ref):
    o_ref[...] = jnp.dot(a_ref[...], b_ref[...], preferred_element_type=F32).astype(o_ref.dtype)


def _matmul(a, b, out_dtype, tm, tn, name):
    M, K = a.shape
    N = b.shape[1]
    tm, tn = min(tm, M), min(tn, N)
    blk = tm * K * 2 + K * tn * 2 + tm * tn * jnp.dtype(out_dtype).itemsize
    return pl.pallas_call(
        _mm_kernel, grid=(N // tn, M // tm),
        in_specs=[pl.BlockSpec((tm, K), lambda j, i: (i, 0)),
                  pl.BlockSpec((K, tn), lambda j, i: (0, j))],
        out_specs=pl.BlockSpec((tm, tn), lambda j, i: (i, j)),
        out_shape=jax.ShapeDtypeStruct((M, N), out_dtype),
        compiler_params=_cparams(("parallel", "parallel"), blk, tm * tn * 4), name=name)(a, b)


def _dft_tables(n, scale):
    k = jnp.arange(n, dtype=jnp.int32)
    ang = ((k[:, None] * k[None, :]) % n).astype(F32) * (2.0 * math.pi / n)
    return (jnp.cos(ang) * scale).astype(BF16), (jnp.sin(ang) * scale).astype(BF16)


def _dft_chan_kernel(u_ref, cs_ref, p_ref, q_ref):
    gd = FNET_GROUP_DIM
    for g in range(FNET_GROUPS):
        ug = u_ref[:, g * gd:(g + 1) * gd].astype(BF16)
        r = jnp.dot(ug, cs_ref[...], preferred_element_type=F32)
        p_ref[:, g * gd:(g + 1) * gd] = r[:, :gd].astype(BF16)
        q_ref[:, g * gd:(g + 1) * gd] = r[:, gd:].astype(BF16)


def _dft_seq_kernel(c_ref, s_ref, p_ref, q_ref, o_ref):
    acc = jnp.dot(c_ref[...], p_ref[...], preferred_element_type=F32)
    acc = acc - jnp.dot(s_ref[...], q_ref[...], preferred_element_type=F32)
    o_ref[...] = acc.astype(o_ref.dtype)


def _fourier_branch(h, B, S, tm=512):
    T = B * S
    gd = FNET_GROUP_DIM
    cc, sc = _dft_tables(gd, gd ** -0.5)
    cs = jnp.concatenate([cc, sc], axis=1)
    tm1 = min(tm, T)
    p, q = pl.pallas_call(
        _dft_chan_kernel, grid=(T // tm1,),
        in_specs=[pl.BlockSpec((tm1, FNET_WIDTH), lambda i: (i, OFF_U // FNET_WIDTH)),
                  pl.BlockSpec((gd, 2 * gd), lambda i: (0, 0))],
        out_specs=(pl.BlockSpec((tm1, FNET_WIDTH), lambda i: (i, 0)),) * 2,
        out_shape=(jax.ShapeDtypeStruct((T, FNET_WIDTH), BF16),) * 2,
        compiler_params=_cparams(("parallel",), tm1 * FNET_WIDTH * 8), name="dft_chan")(h, cs)
    cseq, sseq = _dft_tables(S, S ** -0.5)
    tm2 = min(tm, S)
    blk = 2 * tm2 * S * 2 + 2 * S * FNET_WIDTH * 2 + tm2 * FNET_WIDTH * 2
    y = pl.pallas_call(
        _dft_seq_kernel, grid=(S // tm2, B),
        in_specs=[pl.BlockSpec((tm2, S), lambda i, b: (i, 0)),
                  pl.BlockSpec((tm2, S), lambda i, b: (i, 0)),
                  pl.BlockSpec((None, S, FNET_WIDTH), lambda i, b: (b, 0, 0)),
                  pl.BlockSpec((None, S, FNET_WIDTH), lambda i, b: (b, 0, 0))],
        out_specs=pl.BlockSpec((None, tm2, FNET_WIDTH), lambda i, b: (b, i, 0)),
        out_shape=jax.ShapeDtypeStruct((B, S, FNET_WIDTH), BF16),
        compiler_params=_cparams(("parallel", "parallel"), blk, tm2 * FNET_WIDTH * 8),
        name="dft_seq")(cseq, sseq, p.reshape(B, S, FNET_WIDTH), q.reshape(B, S, FNET_WIDTH))
    return y.reshape(T, FNET_WIDTH)


def _hgrn_pivots(b, row):
    C = HGRN_CHUNK
    pivots = []
    for c in (32, 16, 8, 4):
        n = C // (2 * c)
        piv = b.reshape(n, 2 * c, LANES)[:, c - 1:c, :]
        pivots.append(jnp.broadcast_to(piv, (n, 2 * c, LANES)).reshape(C, LANES))
    r4 = row & 3
    up1, dn1, dn2 = pltpu.roll(b, C - 1, 0), pltpu.roll(b, 1, 0), pltpu.roll(b, 2, 0)
    pivots.append(jnp.where(r4 == 0, up1, jnp.where(r4 == 1, b, jnp.where(r4 == 2, dn1, dn2))))
    pivots.append(jnp.where((row & 1) == 0, b, dn1))
    return pivots


def _hgrn_kernel(q_ref, z_ref, v_ref, lb_ref, o_ref, st_ref, *, n_chunks):
    C = HGRN_CHUNK

    @pl.when(pl.program_id(2) == 0)
    def _():
        st_ref[...] = jnp.zeros_like(st_ref)

    lb = lb_ref[...]
    log_lb = jnp.log(jnp.maximum(lb, LB_FLOOR))
    log_1m = jnp.log1p(-lb)
    one_m = 1.0 - lb
    row = lax.broadcasted_iota(jnp.int32, (C, LANES), 0)
    tt = lax.broadcasted_iota(jnp.int32, (C, C), 0)
    ss = lax.broadcasted_iota(jnp.int32, (C, C), 1)
    level_masks = []
    for sh in (5, 4, 3, 2, 1, 0):
        tb, sb = tt >> sh, ss >> sh
        level_masks.append(sb == jnp.where((tb & 1) == 1, tb - 1, -1))
    diag_mask = tt == ss

    def chunk(i, carry):
        r0 = pl.multiple_of(i * C, C)
        q = q_ref[pl.ds(r0, C), :]
        z = z_ref[pl.ds(r0, C), :]
        v = v_ref[pl.ds(r0, C), :].astype(BF16)
        log_sig = jnp.minimum(z, 0.0) - jnp.log1p(jnp.exp(-jnp.abs(z)))
        t = log_1m + log_sig
        logf = jnp.maximum(log_lb, t) + jnp.log1p(jnp.exp(-jnp.abs(log_lb - t)))
        k = one_m / (1.0 + jnp.exp(z))
        b = logf
        for s in (1, 2, 4, 8, 16, 32):
            b = b + jnp.where(row >= s, pltpu.roll(b, s, 0), 0.0)
        qb, kb = q.astype(BF16), k.astype(BF16)
        a = jnp.where(diag_mask, lax.dot_general(qb, kb, NT_DIMS, preferred_element_type=F32), 0.0)
        for piv, mask in zip(_hgrn_pivots(b, row), level_masks):
            e = jnp.exp(-jnp.abs(b - piv))
            al = lax.dot_general((q * e).astype(BF16), (k * e).astype(BF16), NT_DIMS,
                                 preferred_element_type=F32)
            a = a + jnp.where(mask, al, 0.0)
        st = st_ref[...]
        b_last = b[C - 1:C, :]
        o = jnp.dot(a.astype(BF16), v, preferred_element_type=F32)
        o = o + lax.dot_general((q * jnp.exp(b)).astype(BF16), st.astype(BF16), NT_DIMS,
                                preferred_element_type=F32)
        o_ref[pl.ds(r0, C), :] = o
        k_tail = (k * jnp.exp(b_last - b)).astype(BF16)
        st_ref[...] = st * jnp.exp(b_last) + lax.dot_general(v, k_tail, TN_DIMS,
                                                             preferred_element_type=F32)
        return carry

    lax.fori_loop(0, n_chunks, chunk, 0)


def _hgrn_scan(src, off_q, off_z, off_v, lb, B, S, rb=512):
    rb = min(rb, S)
    spec = lambda off: pl.BlockSpec((None, rb, LANES), lambda b, h, c: (b, c, off + h))
    return pl.pallas_call(
        functools.partial(_hgrn_kernel, n_chunks=rb // HGRN_CHUNK),
        grid=(B, HGRN_HEADS, S // rb),
        in_specs=[spec(off_q), spec(off_z), spec(off_v),
                  pl.BlockSpec((None, 1, LANES), lambda b, h, c: (h, 0, 0))],
        out_specs=pl.BlockSpec((None, rb, LANES), lambda b, h, c: (b, c, h)),
        out_shape=jax.ShapeDtypeStruct((B, S, HGRN_DK), F32),
        scratch_shapes=[pltpu.VMEM((LANES, LANES), F32)],
        compiler_params=_cparams(("parallel", "parallel", "arbitrary"), 4 * rb * LANES * 4, 2 * MIB),
        name="hgrn_scan")(src, src, src, lb.reshape(HGRN_HEADS, 1, LANES).astype(F32))


def _rope(t, cos, sin, perm):
    rot = jnp.dot(t.astype(BF16), perm, preferred_element_type=F32)
    return t * cos + rot * sin


def _attn_kernel(q_ref, k_ref, v_ref, cos_ref, sin_ref, o_ref, *, L, radius):
    blk = ATTN_BLOCK
    width = blk + 2 * radius
    i = pl.program_id(1)
    q0 = pl.multiple_of(i * blk, blk)
    start = pl.multiple_of(jnp.clip(i * blk - radius, 0, L - width), 8)
    pr = lax.broadcasted_iota(jnp.int32, (ATTN_DH, ATTN_DH), 0)
    pc = lax.broadcasted_iota(jnp.int32, (ATTN_DH, ATTN_DH), 1)
    half = ROPE_DIM // 2
    src = jnp.where(pc < half, pc + half, jnp.where(pc < ROPE_DIM, pc - half, -1))
    perm = jnp.where(pr == src, 1.0, 0.0).astype(BF16)
    qr = _rope(q_ref[...], cos_ref[pl.ds(q0, blk), :], sin_ref[pl.ds(q0, blk), :], perm)
    kr = _rope(k_ref[pl.ds(start, width), :], cos_ref[pl.ds(start, width), :],
               sin_ref[pl.ds(start, width), :], perm)
    s = lax.dot_general(qr.astype(BF16), kr.astype(BF16), NT_DIMS, preferred_element_type=F32)
    s = s * (1.0 / math.sqrt(ATTN_DH))
    qpos = q0 + lax.broadcasted_iota(jnp.int32, (blk, width), 0)
    kpos = start + lax.broadcasted_iota(jnp.int32, (blk, width), 1)
    s = jnp.where(jnp.abs(kpos - qpos) <= radius, s, NEG_BIG)
    m = jnp.max(s, axis=-1, keepdims=True)
    p = jnp.exp(s - m)
    den = jnp.sum(p, axis=-1, keepdims=True)
    o = jnp.dot(p.astype(BF16), v_ref[pl.ds(start, width), :].astype(BF16),
                preferred_element_type=F32)
    lse = m + jnp.log(den)
    lane = lax.broadcasted_iota(jnp.int32, (blk, LANES), 1)
    o_ref[...] = jnp.where(lane < ATTN_DH, o / den, lse)


def _band_attention(q, k, v, cos, sin, dil, radius):
    N, L, _ = q.shape
    res = lambda n, i: ((n // ATTN_G) % dil, 0, 0)
    blk = ATTN_BLOCK * LANES * 4 * 2 + 4 * L * LANES * 4
    return pl.pallas_call(
        functools.partial(_attn_kernel, L=L, radius=radius), grid=(N, L // ATTN_BLOCK),
        in_specs=[pl.BlockSpec((None, ATTN_BLOCK, ATTN_DH), lambda n, i: (n, i, 0)),
                  pl.BlockSpec((None, L, ATTN_DH), lambda n, i: (n, 0, 0)),
                  pl.BlockSpec((None, L, LANES), lambda n, i: (n, 0, 0)),
                  pl.BlockSpec((None, L, ATTN_DH), res),
                  pl.BlockSpec((None, L, ATTN_DH), res)],
        out_specs=pl.BlockSpec((None, ATTN_BLOCK, LANES), lambda n, i: (n, i, 0)),
        out_shape=jax.ShapeDtypeStruct((N, L, LANES), F32),
        compiler_params=_cparams(("parallel", "parallel"), blk, 2 * MIB),
        name=f"band_attn_d{dil}")(q, k, v, cos, sin)


def _attention_branch(h3, B, S):
    pos = jnp.arange(S, dtype=F32)
    inv_freq = ROPE_THETA ** (-jnp.arange(0, ROPE_DIM, 2, dtype=F32) / ROPE_DIM)
    ang = pos[:, None] * inv_freq[None, :]
    cos, sin = jnp.cos(ang), jnp.sin(ang)
    rest = ATTN_DH - ROPE_DIM
    cos_f = jnp.concatenate([cos, cos, jnp.ones((S, rest), F32)], axis=1)
    sin_f = jnp.concatenate([-sin, sin, jnp.zeros((S, rest), F32)], axis=1)
    outs = []
    for gi, (window, dil) in enumerate(ATTN_GROUPS):
        radius = window // (2 * dil)
        L = S // dil

        def to_sub(off):
            t = h3[:, :, off + gi * ATTN_G * ATTN_DH: off + (gi + 1) * ATTN_G * ATTN_DH]
            t = t.reshape(B, L, dil, ATTN_G, ATTN_DH).transpose(0, 2, 3, 1, 4)
            return t.reshape(B * dil * ATTN_G, L, ATTN_DH)

        tab = lambda t: t.reshape(L, dil, ATTN_DH).transpose(1, 0, 2)
        v = jnp.pad(to_sub(OFF_VA), ((0, 0), (0, 0), (0, LANES - ATTN_DH)))
        o = _band_attention(to_sub(OFF_QA), to_sub(OFF_KA), v, tab(cos_f), tab(sin_f), dil, radius)
        o = o.reshape(B, dil, ATTN_G, L, LANES).transpose(0, 3, 1, 2, 4)
        outs.append(o.reshape(B * S, ATTN_G * LANES))
    return outs


def _sigmoid(x):
    return 1.0 / (1.0 + jnp.exp(-x))


def _prep_kernel(of_ref, ob_ref, g_ref, ng_ref, a0_ref, a1_ref, a2_ref, yh_ref, ya_ref):
    for hd in range(HGRN_HEADS):
        sl = slice(hd * LANES, (hd + 1) * LANES)
        o = of_ref[:, sl] + ob_ref[:, sl]
        o = o * lax.rsqrt(jnp.mean(o * o, axis=-1, keepdims=True) + HGRN_NORM_EPS)
        g = g_ref[:, sl]
        yh_ref[:, sl] = (o * ng_ref[:, sl] * (g * _sigmoid(g))).astype(BF16)
    lane = lax.broadcasted_iota(jnp.int32, (of_ref.shape[0], LANES), 1)
    for hd in range(ATTN_G):
        sl = slice(hd * LANES, (hd + 1) * LANES)
        xs = [r[:, sl] for r in (a0_ref, a1_ref, a2_ref)]
        mx = jnp.maximum(jnp.maximum(xs[0], xs[1]), xs[2])
        es = [jnp.exp(x - mx) for x in xs]
        inv = 1.0 / (es[0] + es[1] + es[2])
        out = sum(pltpu.roll(e * inv, ATTN_DH, 1) * x for e, x in zip(es, xs))
        ya_ref[:, sl] = jnp.where(lane < ATTN_DH, out, 0.0).astype(BF16)


def _branch_prep(o_f, o_b, g_h, norm_g, attn, tm=256):
    T = o_f.shape[0]
    wa = ATTN_G * LANES
    rows = lambda w: pl.BlockSpec((tm, w), lambda i: (i, 0))
    blk = tm * (3 * HGRN_DK + 3 * wa) * 4 + tm * (HGRN_DK + wa) * 2
    return pl.pallas_call(
        _prep_kernel, grid=(T // tm,),
        in_specs=[rows(HGRN_DK), rows(HGRN_DK), rows(HGRN_DK),
                  pl.BlockSpec((1, HGRN_DK), lambda i: (0, 0)), rows(wa), rows(wa), rows(wa)],
        out_specs=(rows(HGRN_DK), rows(wa)),
        out_shape=(jax.ShapeDtypeStruct((T, HGRN_DK), BF16), jax.ShapeDtypeStruct((T, wa), BF16)),
        compiler_params=_cparams(("parallel",), blk, 2 * MIB), name="branch_prep")(
            o_f, o_b, g_h, norm_g.reshape(1, HGRN_DK).astype(F32), *attn)


def _merge_kernel(yf_ref, yh_ref, ya_ref, wf_ref, wh_ref, wa_ref, g0_ref, g1_ref, g2_ref, o_ref):
    dot = lambda y, w: jnp.dot(y[...], w[...], preferred_element_type=F32)
    acc = _sigmoid(g0_ref[...]) * dot(yf_ref, wf_ref)
    acc = acc + _sigmoid(g1_ref[...]) * dot(yh_ref, wh_ref)
    acc = acc + _sigmoid(g2_ref[...]) * dot(ya_ref, wa_ref)
    o_ref[...] = acc.astype(o_ref.dtype)


def _gated_merge(y_f, y_h, y_a, w_f, w_h, w_a, h, D, tm=512, tn=512):
    T = y_f.shape[0]
    tm = min(tm, T)
    kf, kh, ka = y_f.shape[1], y_h.shape[1], y_a.shape[1]
    rows = lambda w: pl.BlockSpec((tm, w), lambda j, i: (i, 0))
    cols = lambda kk: pl.BlockSpec((kk, tn), lambda j, i: (0, j))
    gate = lambda br: pl.BlockSpec((tm, tn), lambda j, i: (i, (OFF_GATE + br * D) // tn + j))
    blk = tm * (kf + kh + ka) * 2 + (kf + kh + ka) * tn * 2 + 3 * tm * tn * 4 + tm * tn * 2
    return pl.pallas_call(
        _merge_kernel, grid=(D // tn, T // tm),
        in_specs=[rows(kf), rows(kh), rows(ka), cols(kf), cols(kh), cols(ka),
                  gate(0), gate(1), gate(2)],
        out_specs=pl.BlockSpec((tm, tn), lambda j, i: (i, j)),
        out_shape=jax.ShapeDtypeStruct((T, D), BF16),
        compiler_params=_cparams(("parallel", "parallel"), blk, 4 * tm * tn * 4),
        name="gated_merge")(y_f, y_h, y_a, w_f, w_h, w_a, h, h, h)


def _top_rows(s, n):
    rowi = lax.broadcasted_iota(jnp.int32, (n, s.shape[1]), 0)
    top = jnp.full((n, s.shape[1]), NEG_BIG, F32)
    for r in range(n):
        m = jnp.max(s, axis=0, keepdims=True)
        top = jnp.where(rowi == r, m, top)
        s = jnp.where(s == m, NEG_BIG, s)
    return top, jnp.max(s, axis=0, keepdims=True)


def _peer_prep_kernel(q_ref, keys_ref, th_ref, a_ref, s2_ref, bp_ref):
    K = PEER_TOPK
    nk = PEER_N_KEYS
    for hd in range(PEER_HEADS):
        sc = []
        for p in range(2):
            qb = q_ref[:, (2 * hd + p) * nk:(2 * hd + p + 1) * nk]
            sc.append(lax.dot_general(keys_ref[p], qb, NT_DIMS, precision=lax.Precision.HIGHEST,
                                      preferred_element_type=F32))
        s1, s2 = sc
        r1, x1 = _top_rows(s1, K)
        r2, x2 = _top_rows(s2, K)
        cand = jnp.concatenate([r1[a:a + 1, :] + r2 for a in range(K)], axis=0)
        ctop, c17 = _top_rows(cand, K)
        c16 = ctop[K - 1:K, :]
        c17 = jnp.maximum(c17, jnp.maximum(x1 + r2[0:1, :], r1[0:1, :] + x2))
        tau = 0.5 * (c16 + c17)
        m1, m2 = r1[0:1, :], r2[0:1, :]
        z = jnp.zeros_like(tau)
        for a in range(K):
            ra = r1[a:a + 1, :]
            z = z + jnp.exp(ra - m1) * jnp.sum(
                jnp.where(r2 >= tau - ra, jnp.exp(r2 - m2), 0.0), axis=0, keepdims=True)
        th_ref[hd] = tau - s1
        a_ref[hd] = jnp.exp(s1 - m1)
        s2_ref[hd] = s2
        bp_ref[hd] = jnp.exp(s2 - m2) / z


def _peer_prep(qp, sub_keys, tb=256):
    T, W = qp.shape
    tb = min(tb, T)
    nk = PEER_N_KEYS
    out = pl.BlockSpec((PEER_HEADS, nk, tb), lambda i: (0, 0, i))
    blk = tb * W * 4 + 2 * nk * nk * 4 + 4 * PEER_HEADS * nk * tb * 4
    return pl.pallas_call(
        _peer_prep_kernel, grid=(T // tb,),
        in_specs=[pl.BlockSpec((tb, W), lambda i: (i, 0)),
                  pl.BlockSpec((2, nk, sub_keys.shape[2]), lambda i: (0, 0, 0))],
        out_specs=(out,) * 4,
        out_shape=(jax.ShapeDtypeStruct((PEER_HEADS, nk, T), F32),) * 4,
        compiler_params=_cparams(("parallel",), blk, 4 * MIB), name="peer_prep")(qp, sub_keys)


def _peer_main_kernel(xt_ref, u_ref, vt_ref, th_ref, a_ref, s2_ref, bp_ref, o_ref, p_ref, *, n_sub):
    nk = PEER_N_KEYS
    e = pl.program_id(1)

    @pl.when(e == 0)
    def _():
        o_ref[...] = jnp.zeros_like(o_ref)

    hid = jnp.dot(u_ref[...], xt_ref[...], preferred_element_type=F32)
    for j in range(n_sub):
        i1 = e * n_sub + j
        w = jnp.zeros((nk, hid.shape[1]), F32)
        for hd in range(PEER_HEADS):
            th = th_ref[hd, pl.ds(i1, 1), :]
            a = a_ref[hd, pl.ds(i1, 1), :]
            w = w + jnp.where(s2_ref[hd] >= th, bp_ref[hd], 0.0) * a
        hj = hid[j * nk:(j + 1) * nk, :]
        act = 0.5 * hj * (1.0 + lax.erf(hj * (1.0 / math.sqrt(2.0))))
        p_ref[j * nk:(j + 1) * nk, :] = (act * w).astype(BF16)
    o_ref[...] += jnp.dot(vt_ref[...], p_ref[...], preferred_element_type=F32)


def _peer_main(xt, u, vt, th, a, s2, bp, tb=512, eb=512):
    D, T = xt.shape
    E = u.shape[0]
    tb = min(tb, T)
    nk = PEER_N_KEYS
    tok = pl.BlockSpec((PEER_HEADS, nk, tb), lambda t, e: (0, 0, t))
    blk = D * tb * 2 + 2 * eb * D * 2 + 4 * PEER_HEADS * nk * tb * 4 + D * tb * 4
    return pl.pallas_call(
        functools.partial(_peer_main_kernel, n_sub=eb // nk), grid=(T // tb, E // eb),
        in_specs=[pl.BlockSpec((D, tb), lambda t, e: (0, t)),
                  pl.BlockSpec((eb, D), lambda t, e: (e, 0)),
                  pl.BlockSpec((D, eb), lambda t, e: (0, e)),
                  tok, tok, tok, tok],
        out_specs=pl.BlockSpec((D, tb), lambda t, e: (0, t)),
        out_shape=jax.ShapeDtypeStruct((D, T), F32),
        scratch_shapes=[pltpu.VMEM((eb, tb), BF16)],
        compiler_params=_cparams(("parallel", "arbitrary"), blk, eb * tb * 2 + 2 * eb * tb * 4),
        name="peer_main")(xt, u, vt, th, a, s2, bp)


def _peer_ffn(x_bf, w_query, sub_keys, u, v):
    qp = _matmul(x_bf, w_query.astype(BF16), F32, 1024, 512, "peer_query")
    th, a, s2, bp = _peer_prep(qp, sub_keys.astype(F32))
    yt = _peer_main(x_bf.T, u.astype(BF16), v.T.astype(BF16), th, a, s2, bp)
    return yt.T


def _hybrid_mixer(x_bf, w_in, w_br_f, w_br_h, w_br_a, w_out, lb_f, lb_b, norm_g, B, S):
    T, D = x_bf.shape
    h = _matmul(x_bf, w_in.astype(BF16), F32, 1024, 512, "in_proj")
    h3 = h.reshape(B, S, h.shape[1])
    y_f = _fourier_branch(h, B, S)
    o_f = _hgrn_scan(h3, OFF_Q // LANES, OFF_FF // LANES, OFF_I // LANES, lb_f, B, S)
    hb = jnp.flip(h3[:, :, OFF_FB:OFF_G], axis=1)
    o_b = _hgrn_scan(hb, (OFF_Q - OFF_FB) // LANES, 0, (OFF_I - OFF_FB) // LANES, lb_b, B, S)
    o_b = jnp.flip(o_b, axis=1)
    attn = _attention_branch(h3, B, S)
    y_h, y_a = _branch_prep(o_f.reshape(T, HGRN_DK), o_b.reshape(T, HGRN_DK),
                            h[:, OFF_G:OFF_QA], norm_g, attn)
    w_a = jnp.pad(w_br_a.reshape(ATTN_G, ATTN_DH, D), ((0, 0), (0, LANES - ATTN_DH), (0, 0)))
    merged = _gated_merge(y_f, y_h, y_a, w_br_f.astype(BF16), w_br_h.astype(BF16),
                          w_a.reshape(ATTN_G * LANES, D).astype(BF16), h, D)
    return _matmul(merged, w_out.astype(BF16), F32, 1024, 512, "out_proj")


def kernel(x, emb_ln_g, emb_ln_b, w_in, w_br_fourier, w_br_hgrn, w_br_attn, w_out,
           hgrn_lb_logits, hgrn_norm_g, ln_mix_g, ln_mix_b, peer_w_query, peer_sub_keys,
           peer_u, peer_v, ln_ffn_g, ln_ffn_b):
    B, S, D = x.shape
    T = B * S
    depth = w_in.shape[0]
    alpha = (2.0 * depth) ** 0.25
    lb_p = jax.nn.softmax(hgrn_lb_logits.astype(F32), axis=1)
    lb = jnp.cumsum(lb_p, axis=1) - lb_p[:, :1]
    xf, xb = _layer_norm(x.reshape(T, D), emb_ln_g, emb_ln_b)
    for l in range(depth):
        mix = _hybrid_mixer(xb, w_in[l], w_br_fourier[l], w_br_hgrn[l], w_br_attn[l], w_out[l],
                            lb[0, l], lb[1, l], hgrn_norm_g[l], B, S)
        xf, xb = _layer_norm(xf, ln_mix_g[l], ln_mix_b[l], res=mix, alpha=alpha)
        ffn = _peer_ffn(xb, peer_w_query[l], peer_sub_keys[l], peer_u[l], peer_v[l])
        xf, xb = _layer_norm(xf, ln_ffn_g[l], ln_ffn_b[l], res=ffn, alpha=alpha)
    return xf.reshape(B, S, D)
```

```python
import functools
import math

import jax
import jax.numpy as jnp
from jax import lax
from jax.experimental import pallas as pl
from jax.experimental.pallas import tpu as pltpu

F32 = jnp.float32
BF16 = jnp.bfloat16

V7X_VMEM_BYTES = 64 * 1024 * 1024
LANES = 128
MIB = 1024 * 1024

FNET_GROUPS = 4
FNET_GROUP_DIM = 128
FNET_WIDTH = FNET_GROUPS * FNET_GROUP_DIM
HGRN_HEADS = 6
HGRN_DK = HGRN_HEADS * 128
HGRN_CHUNK = 64
HGRN_NORM_EPS = 1e-6
LB_FLOOR = 1e-30
ATTN_GROUPS = ((128, 1), (512, 4), (2048, 16))
ATTN_G = 4
ATTN_DH = 64
ATTN_HEADS = ATTN_G * len(ATTN_GROUPS)
ATTN_WIDTH = ATTN_HEADS * ATTN_DH
ATTN_BLOCK = 128
ATTN_BLOCKS_PER_STEP = 4
ROPE_THETA = 500000.0
ROPE_DIM = ATTN_DH // 4
NEG_BIG = -1e30
N_BRANCHES = 3
PEER_HEADS = 8
PEER_N_KEYS = 128
PEER_TOPK = 16
LN_EPS = 1e-5

OFF_U = 0
OFF_FF = OFF_U + FNET_WIDTH
OFF_FB = OFF_FF + HGRN_DK
OFF_Q = OFF_FB + HGRN_DK
OFF_I = OFF_Q + HGRN_DK
OFF_G = OFF_I + HGRN_DK
OFF_QA = OFF_G + HGRN_DK
OFF_KA = OFF_QA + ATTN_WIDTH
OFF_VA = OFF_KA + ATTN_WIDTH
OFF_GATE = OFF_VA + ATTN_WIDTH

NT_DIMS = (((1,), (1,)), ((), ()))
TN_DIMS = (((0,), (0,)), ((), ()))


def _cparams(semantics, block_bytes, temp_bytes=0):
    need = 2 * block_bytes + temp_bytes + 4 * MIB
    limit = int(min(V7X_VMEM_BYTES - 8 * MIB, max(need, 16 * MIB)))
    return pltpu.CompilerParams(dimension_semantics=semantics, vmem_limit_bytes=limit)


def _ln_rows(v, g, b):
    mu = jnp.mean(v, axis=-1, keepdims=True)
    d = v - mu
    var = jnp.mean(d * d, axis=-1, keepdims=True)
    return d * lax.rsqrt(var + LN_EPS) * g + b


def _ln_kernel(x_ref, g_ref, b_ref, of_ref, ob_ref):
    y = _ln_rows(x_ref[...], g_ref[...], b_ref[...])
    of_ref[...] = y
    ob_ref[...] = y.astype(BF16)


def _ln_res_kernel(x_ref, y_ref, g_ref, b_ref, of_ref, ob_ref, *, alpha):
    y = _ln_rows(alpha * x_ref[...] + y_ref[...], g_ref[...], b_ref[...])
    of_ref[...] = y
    ob_ref[...] = y.astype(BF16)


def _layer_norm(x, g, b, res=None, alpha=1.0, tm=256):
    T, D = x.shape
    row = pl.BlockSpec((tm, D), lambda i: (i, 0))
    vec = pl.BlockSpec((1, D), lambda i: (0, 0))
    out_shape = (jax.ShapeDtypeStruct((T, D), F32), jax.ShapeDtypeStruct((T, D), BF16))
    g2, b2 = g.reshape(1, D).astype(F32), b.reshape(1, D).astype(F32)
    blk = tm * D * 4
    if res is None:
        return pl.pallas_call(
            _ln_kernel, grid=(T // tm,), in_specs=[row, vec, vec], out_specs=(row, row),
            out_shape=out_shape, compiler_params=_cparams(("parallel",), 3 * blk, 2 * blk),
            name="ln")(x, g2, b2)
    return pl.pallas_call(
        functools.partial(_ln_res_kernel, alpha=alpha), grid=(T // tm,),
        in_specs=[row, row, vec, vec], out_specs=(row, row), out_shape=out_shape,
        compiler_params=_cparams(("parallel",), 4 * blk, 2 * blk), name="ln_res")(x, res, g2, b2)


def _mm_kernel(a_ref, b_ref, o_ref, wb_ref):
    @pl.when(pl.program_id(1) == 0)
    def _():
        wb_ref[...] = b_ref[...].astype(BF16)

    o_ref[...] = jnp.dot(a_ref[...], wb_ref[...], preferred_element_type=F32).astype(o_ref.dtype)


def _matmul(a, b, out_dtype, tm, tn, name):
    M, K = a.shape
    N = b.shape[1]
    tm, tn = min(tm, M), min(tn, N)
    blk = tm * K * 2 + K * tn * 4 + tm * tn * jnp.dtype(out_dtype).itemsize
    return pl.pallas_call(
        _mm_kernel, grid=(N // tn, M // tm),
        in_specs=[pl.BlockSpec((tm, K), lambda j, i: (i, 0)),
                  pl.BlockSpec((K, tn), lambda j, i: (0, j))],
        out_specs=pl.BlockSpec((tm, tn), lambda j, i: (i, j)),
        out_shape=jax.ShapeDtypeStruct((M, N), out_dtype),
        scratch_shapes=[pltpu.VMEM((K, tn), BF16)],
        compiler_params=_cparams(("parallel", "arbitrary"), blk, tm * tn * 4 + K * tn * 2),
        name=name)(a, b)


DFT_SPLIT = 64


def _dft_angles(rows, n, period):
    k = jnp.arange(n, dtype=jnp.int32)
    r = jnp.arange(rows, dtype=jnp.int32)
    return ((r[:, None] * k[None, :]) % period).astype(F32) * (2.0 * math.pi / period)


def _dft_tables(n, scale):
    if n <= DFT_SPLIT * 8:
        ang = _dft_angles(n, n, n)
        c, s = jnp.cos(ang), jnp.sin(ang)
    else:
        hi = _dft_angles(n // DFT_SPLIT, n, n // DFT_SPLIT)
        lo = _dft_angles(DFT_SPLIT, n, n)
        ch, sh = jnp.cos(hi)[:, None, :], jnp.sin(hi)[:, None, :]
        cl, sl = jnp.cos(lo)[None, :, :], jnp.sin(lo)[None, :, :]
        c = (ch * cl - sh * sl).reshape(n, n)
        s = (sh * cl + ch * sl).reshape(n, n)
    return (c * scale).astype(BF16), (s * scale).astype(BF16)


def _dft_chan_kernel(u_ref, cs_ref, p_ref, q_ref):
    gd = FNET_GROUP_DIM
    for g in range(FNET_GROUPS):
        ug = u_ref[:, g * gd:(g + 1) * gd].astype(BF16)
        r = jnp.dot(ug, cs_ref[...], preferred_element_type=F32)
        p_ref[:, g * gd:(g + 1) * gd] = r[:, :gd].astype(BF16)
        q_ref[:, g * gd:(g + 1) * gd] = r[:, gd:].astype(BF16)


def _dft_seq_kernel(c_ref, s_ref, p_ref, q_ref, o_ref):
    acc = jnp.dot(c_ref[...], p_ref[...], preferred_element_type=F32)
    acc = acc - jnp.dot(s_ref[...], q_ref[...], preferred_element_type=F32)
    o_ref[...] = acc.astype(o_ref.dtype)


def _fourier_branch(h, B, S, tm=512):
    T = B * S
    gd = FNET_GROUP_DIM
    cc, sc = _dft_tables(gd, gd ** -0.5)
    cs = jnp.concatenate([cc, sc], axis=1)
    tm1 = min(tm, T)
    p, q = pl.pallas_call(
        _dft_chan_kernel, grid=(T // tm1,),
        in_specs=[pl.BlockSpec((tm1, FNET_WIDTH), lambda i: (i, OFF_U // FNET_WIDTH)),
                  pl.BlockSpec((gd, 2 * gd), lambda i: (0, 0))],
        out_specs=(pl.BlockSpec((tm1, FNET_WIDTH), lambda i: (i, 0)),) * 2,
        out_shape=(jax.ShapeDtypeStruct((T, FNET_WIDTH), BF16),) * 2,
        compiler_params=_cparams(("parallel",), tm1 * FNET_WIDTH * 8), name="dft_chan")(h, cs)
    cseq, sseq = _dft_tables(S, S ** -0.5)
    tm2 = min(tm, S)
    blk = 2 * tm2 * S * 2 + 2 * S * FNET_WIDTH * 2 + tm2 * FNET_WIDTH * 2
    y = pl.pallas_call(
        _dft_seq_kernel, grid=(S // tm2, B),
        in_specs=[pl.BlockSpec((tm2, S), lambda i, b: (i, 0)),
                  pl.BlockSpec((tm2, S), lambda i, b: (i, 0)),
                  pl.BlockSpec((None, S, FNET_WIDTH), lambda i, b: (b, 0, 0)),
                  pl.BlockSpec((None, S, FNET_WIDTH), lambda i, b: (b, 0, 0))],
        out_specs=pl.BlockSpec((None, tm2, FNET_WIDTH), lambda i, b: (b, i, 0)),
        out_shape=jax.ShapeDtypeStruct((B, S, FNET_WIDTH), BF16),
        compiler_params=_cparams(("parallel", "parallel"), blk, tm2 * FNET_WIDTH * 8),
        name="dft_seq")(cseq, sseq, p.reshape(B, S, FNET_WIDTH), q.reshape(B, S, FNET_WIDTH))
    return y.reshape(T, FNET_WIDTH)


def _hgrn_pivots(b, row, reverse):
    C = HGRN_CHUNK
    pivots = []
    for c in (32, 16, 8, 4):
        n = C // (2 * c)
        at = c if reverse else c - 1
        piv = b.reshape(n, 2 * c, LANES)[:, at:at + 1, :]
        pivots.append(jnp.broadcast_to(piv, (n, 2 * c, LANES)).reshape(C, LANES))
    r4 = row & 3
    up1, up2 = pltpu.roll(b, C - 1, 0), pltpu.roll(b, C - 2, 0)
    dn1, dn2 = pltpu.roll(b, 1, 0), pltpu.roll(b, 2, 0)
    if reverse:
        pivots.append(jnp.where(r4 == 0, up2, jnp.where(r4 == 1, up1, jnp.where(r4 == 2, b, dn1))))
        pivots.append(jnp.where((row & 1) == 0, up1, b))
    else:
        pivots.append(jnp.where(r4 == 0, up1, jnp.where(r4 == 1, b, jnp.where(r4 == 2, dn1, dn2))))
        pivots.append(jnp.where((row & 1) == 0, b, dn1))
    return pivots


def _hgrn_kernel(q_ref, z_ref, v_ref, lb_ref, o_ref, st_ref, *, n_chunks, reverse):
    C = HGRN_CHUNK

    @pl.when(pl.program_id(2) == 0)
    def _():
        st_ref[...] = jnp.zeros_like(st_ref)

    lb = lb_ref[...]
    log_lb = jnp.log(jnp.maximum(lb, LB_FLOOR))
    log_1m = jnp.log1p(-lb)
    one_m = 1.0 - lb
    row = lax.broadcasted_iota(jnp.int32, (C, LANES), 0)
    tt = lax.broadcasted_iota(jnp.int32, (C, C), 0)
    ss = lax.broadcasted_iota(jnp.int32, (C, C), 1)
    src, dst = (tt, ss) if reverse else (ss, tt)
    level_masks = []
    for sh in (5, 4, 3, 2, 1, 0):
        sb, db = src >> sh, dst >> sh
        level_masks.append(sb == jnp.where((db & 1) == 1, db - 1, -1))
    diag_mask = tt == ss
    last = 0 if reverse else C - 1

    def chunk(i, carry):
        ci = n_chunks - 1 - i if reverse else i
        r0 = pl.multiple_of(ci * C, C)
        q = q_ref[pl.ds(r0, C), :]
        z = z_ref[pl.ds(r0, C), :]
        v = v_ref[pl.ds(r0, C), :].astype(BF16)
        log_sig = jnp.minimum(z, 0.0) - jnp.log1p(jnp.exp(-jnp.abs(z)))
        t = log_1m + log_sig
        logf = jnp.maximum(log_lb, t) + jnp.log1p(jnp.exp(-jnp.abs(log_lb - t)))
        k = one_m / (1.0 + jnp.exp(z))
        b = logf
        for s in (1, 2, 4, 8, 16, 32):
            if reverse:
                b = b + jnp.where(row < C - s, pltpu.roll(b, C - s, 0), 0.0)
            else:
                b = b + jnp.where(row >= s, pltpu.roll(b, s, 0), 0.0)
        qb, kb = q.astype(BF16), k.astype(BF16)
        a = jnp.where(diag_mask, lax.dot_general(qb, kb, NT_DIMS, preferred_element_type=F32), 0.0)
        for piv, mask in zip(_hgrn_pivots(b, row, reverse), level_masks):
            e = jnp.exp(-jnp.abs(b - piv))
            al = lax.dot_general((q * e).astype(BF16), (k * e).astype(BF16), NT_DIMS,
                                 preferred_element_type=F32)
            a = a + jnp.where(mask, al, 0.0)
        st = st_ref[...]
        b_last = b[last:last + 1, :]
        o = jnp.dot(a.astype(BF16), v, preferred_element_type=F32)
        o = o + lax.dot_general((q * jnp.exp(b)).astype(BF16), st.astype(BF16), NT_DIMS,
                                preferred_element_type=F32)
        o_ref[pl.ds(r0, C), :] = o
        k_tail = (k * jnp.exp(b_last - b)).astype(BF16)
        st_ref[...] = st * jnp.exp(b_last) + lax.dot_general(v, k_tail, TN_DIMS,
                                                             preferred_element_type=F32)
        return carry

    lax.fori_loop(0, n_chunks, chunk, 0, unroll=2)


def _hgrn_scan(src, off_q, off_z, off_v, lb, B, S, reverse, rb=512):
    rb = min(rb, S)
    nblk = S // rb
    blk_idx = (lambda c: nblk - 1 - c) if reverse else (lambda c: c)
    spec = lambda off: pl.BlockSpec((None, rb, LANES), lambda b, h, c: (b, blk_idx(c), off + h))
    return pl.pallas_call(
        functools.partial(_hgrn_kernel, n_chunks=rb // HGRN_CHUNK, reverse=reverse),
        grid=(B, HGRN_HEADS, nblk),
        in_specs=[spec(off_q), spec(off_z), spec(off_v),
                  pl.BlockSpec((None, 1, LANES), lambda b, h, c: (h, 0, 0))],
        out_specs=pl.BlockSpec((None, rb, LANES), lambda b, h, c: (b, blk_idx(c), h)),
        out_shape=jax.ShapeDtypeStruct((B, S, HGRN_DK), F32),
        scratch_shapes=[pltpu.VMEM((LANES, LANES), F32)],
        compiler_params=_cparams(("parallel", "parallel", "arbitrary"), 4 * rb * LANES * 4, 2 * MIB),
        name="hgrn_scan_rev" if reverse else "hgrn_scan")(
            src, src, src, lb.reshape(HGRN_HEADS, 1, LANES).astype(F32))


def _rope(t, cos, sin, perm):
    rot = jnp.dot(t.astype(BF16), perm, preferred_element_type=F32)
    return t * cos + rot * sin


def _attn_kernel(q_ref, k_ref, v_ref, cos_ref, sin_ref, o_ref, *, L, radius, nseq, nblk):
    blk = ATTN_BLOCK
    width = blk + 2 * radius
    pr = lax.broadcasted_iota(jnp.int32, (ATTN_DH, ATTN_DH), 0)
    pc = lax.broadcasted_iota(jnp.int32, (ATTN_DH, ATTN_DH), 1)
    half = ROPE_DIM // 2
    src = jnp.where(pc < half, pc + half, jnp.where(pc < ROPE_DIM, pc - half, -1))
    perm = jnp.where(pr == src, 1.0, 0.0).astype(BF16)
    row = lax.broadcasted_iota(jnp.int32, (blk, width), 0)
    col = lax.broadcasted_iota(jnp.int32, (blk, width), 1)
    lane = lax.broadcasted_iota(jnp.int32, (blk, LANES), 1)
    for n in range(nseq):
        for j in range(nblk):
            q0 = pl.multiple_of((pl.program_id(1) * nblk + j) * blk, blk)
            start = pl.multiple_of(jnp.clip(q0 - radius, 0, L - width), 8)
            qr = _rope(q_ref[n, j * blk:(j + 1) * blk, :], cos_ref[pl.ds(q0, blk), :],
                       sin_ref[pl.ds(q0, blk), :], perm)
            kr = _rope(k_ref[n, pl.ds(start, width), :], cos_ref[pl.ds(start, width), :],
                       sin_ref[pl.ds(start, width), :], perm)
            s = lax.dot_general(qr.astype(BF16), kr.astype(BF16), NT_DIMS,
                                preferred_element_type=F32)
            s = s * (1.0 / math.sqrt(ATTN_DH))
            s = jnp.where(jnp.abs(col - row + (start - q0)) <= radius, s, NEG_BIG)
            m = jnp.max(s, axis=-1, keepdims=True)
            p = jnp.exp(s - m)
            den = jnp.sum(p, axis=-1, keepdims=True)
            o = jnp.dot(p.astype(BF16), v_ref[n, pl.ds(start, width), :].astype(BF16),
                        preferred_element_type=F32)
            lse = m + jnp.log(den)
            o_ref[n, j * blk:(j + 1) * blk, :] = jnp.where(lane < ATTN_DH, o / den, lse)


def _band_attention(q, k, v, cos, sin, dil, radius):
    N, L, _ = q.shape
    nblk = min(ATTN_BLOCKS_PER_STEP, L // ATTN_BLOCK)
    nseq = ATTN_BLOCKS_PER_STEP // nblk
    rows = nblk * ATTN_BLOCK
    res = lambda n, i: ((n * nseq // ATTN_G) % dil, 0, 0)
    blk = nseq * (rows * LANES * 4 * 2 + 2 * L * LANES * 4) + 2 * L * LANES * 4
    return pl.pallas_call(
        functools.partial(_attn_kernel, L=L, radius=radius, nseq=nseq, nblk=nblk),
        grid=(N // nseq, L // rows),
        in_specs=[pl.BlockSpec((nseq, rows, ATTN_DH), lambda n, i: (n, i, 0)),
                  pl.BlockSpec((nseq, L, ATTN_DH), lambda n, i: (n, 0, 0)),
                  pl.BlockSpec((nseq, L, LANES), lambda n, i: (n, 0, 0)),
                  pl.BlockSpec((None, L, ATTN_DH), res),
                  pl.BlockSpec((None, L, ATTN_DH), res)],
        out_specs=pl.BlockSpec((nseq, rows, LANES), lambda n, i: (n, i, 0)),
        out_shape=jax.ShapeDtypeStruct((N, L, LANES), F32),
        compiler_params=_cparams(("parallel", "parallel"), blk, 4 * MIB),
        name=f"band_attn_d{dil}")(q, k, v, cos, sin)


def _attention_branch(h3, B, S):
    pos = jnp.arange(S, dtype=F32)
    inv_freq = ROPE_THETA ** (-jnp.arange(0, ROPE_DIM, 2, dtype=F32) / ROPE_DIM)
    ang = pos[:, None] * inv_freq[None, :]
    cos, sin = jnp.cos(ang), jnp.sin(ang)
    rest = ATTN_DH - ROPE_DIM
    cos_f = jnp.concatenate([cos, cos, jnp.ones((S, rest), F32)], axis=1)
    sin_f = jnp.concatenate([-sin, sin, jnp.zeros((S, rest), F32)], axis=1)
    outs = []
    for gi, (window, dil) in enumerate(ATTN_GROUPS):
        radius = window // (2 * dil)
        L = S // dil

        def to_sub(off):
            t = h3[:, :, off + gi * ATTN_G * ATTN_DH: off + (gi + 1) * ATTN_G * ATTN_DH]
            t = t.reshape(B, L, dil, ATTN_G, ATTN_DH).transpose(0, 2, 3, 1, 4)
            return t.reshape(B * dil * ATTN_G, L, ATTN_DH)

        tab = lambda t: t.reshape(L, dil, ATTN_DH).transpose(1, 0, 2)
        v = jnp.pad(to_sub(OFF_VA), ((0, 0), (0, 0), (0, LANES - ATTN_DH)))
        o = _band_attention(to_sub(OFF_QA), to_sub(OFF_KA), v, tab(cos_f), tab(sin_f), dil, radius)
        o = o.reshape(B, dil, ATTN_G, L, LANES).transpose(0, 3, 1, 2, 4)
        outs.append(o.reshape(B * S, ATTN_G * LANES))
    return outs


def _sigmoid(x):
    return 1.0 / (1.0 + jnp.exp(-x))


def _prep_kernel(of_ref, ob_ref, g0_ref, g1_ref, g2_ref, ng_ref, a0_ref, a1_ref, a2_ref,
                 yh_ref, ya_ref):
    g_refs = (g0_ref, g1_ref, g2_ref)
    for hd in range(HGRN_HEADS):
        sl = slice(hd * LANES, (hd + 1) * LANES)
        o = of_ref[:, sl] + ob_ref[:, sl]
        o = o * lax.rsqrt(jnp.mean(o * o, axis=-1, keepdims=True) + HGRN_NORM_EPS)
        g = g_refs[hd // 2][:, (hd % 2) * LANES:(hd % 2 + 1) * LANES]
        yh_ref[:, sl] = (o * ng_ref[:, sl] * (g * _sigmoid(g))).astype(BF16)
    lane = lax.broadcasted_iota(jnp.int32, (of_ref.shape[0], LANES), 1)
    for hd in range(ATTN_G):
        sl = slice(hd * LANES, (hd + 1) * LANES)
        xs = [r[:, sl] for r in (a0_ref, a1_ref, a2_ref)]
        mx = jnp.maximum(jnp.maximum(xs[0], xs[1]), xs[2])
        es = [jnp.exp(x - mx) for x in xs]
        inv = 1.0 / (es[0] + es[1] + es[2])
        out = sum(pltpu.roll(e * inv, ATTN_DH, 1) * x for e, x in zip(es, xs))
        ya_ref[:, sl] = jnp.where(lane < ATTN_DH, out, 0.0).astype(BF16)


def _branch_prep(o_f, o_b, h, norm_g, attn, tm=256):
    T = o_f.shape[0]
    wa = ATTN_G * LANES
    rows = lambda w: pl.BlockSpec((tm, w), lambda i: (i, 0))
    blk = tm * (3 * HGRN_DK + 3 * wa) * 4 + tm * (HGRN_DK + wa) * 2
    gw = 2 * LANES
    gate = lambda j: pl.BlockSpec((tm, gw), lambda i: (i, OFF_G // gw + j))
    return pl.pallas_call(
        _prep_kernel, grid=(T // tm,),
        in_specs=[rows(HGRN_DK), rows(HGRN_DK), gate(0), gate(1), gate(2),
                  pl.BlockSpec((1, HGRN_DK), lambda i: (0, 0)), rows(wa), rows(wa), rows(wa)],
        out_specs=(rows(HGRN_DK), rows(wa)),
        out_shape=(jax.ShapeDtypeStruct((T, HGRN_DK), BF16), jax.ShapeDtypeStruct((T, wa), BF16)),
        compiler_params=_cparams(("parallel",), blk, 2 * MIB), name="branch_prep")(
            o_f, o_b, h, h, h, norm_g.reshape(1, HGRN_DK).astype(F32), *attn)


def _merge_kernel(yf_ref, yh_ref, ya_ref, wf_ref, wh_ref, wa_ref, g0_ref, g1_ref, g2_ref, o_ref):
    dot = lambda y, w: jnp.dot(y[...], w[...], preferred_element_type=F32)
    acc = _sigmoid(g0_ref[...]) * dot(yf_ref, wf_ref)
    acc = acc + _sigmoid(g1_ref[...]) * dot(yh_ref, wh_ref)
    acc = acc + _sigmoid(g2_ref[...]) * dot(ya_ref, wa_ref)
    o_ref[...] = acc.astype(o_ref.dtype)


def _gated_merge(y_f, y_h, y_a, w_f, w_h, w_a, h, D, tm=512, tn=512):
    T = y_f.shape[0]
    tm = min(tm, T)
    kf, kh, ka = y_f.shape[1], y_h.shape[1], y_a.shape[1]
    rows = lambda w: pl.BlockSpec((tm, w), lambda j, i: (i, 0))
    cols = lambda kk: pl.BlockSpec((kk, tn), lambda j, i: (0, j))
    gate = lambda br: pl.BlockSpec((tm, tn), lambda j, i: (i, (OFF_GATE + br * D) // tn + j))
    blk = tm * (kf + kh + ka) * 2 + (kf + kh + ka) * tn * 2 + 3 * tm * tn * 4 + tm * tn * 2
    return pl.pallas_call(
        _merge_kernel, grid=(D // tn, T // tm),
        in_specs=[rows(kf), rows(kh), rows(ka), cols(kf), cols(kh), cols(ka),
                  gate(0), gate(1), gate(2)],
        out_specs=pl.BlockSpec((tm, tn), lambda j, i: (i, j)),
        out_shape=jax.ShapeDtypeStruct((T, D), BF16),
        compiler_params=_cparams(("parallel", "parallel"), blk, 4 * tm * tn * 4),
        name="gated_merge")(y_f, y_h, y_a, w_f, w_h, w_a, h, h, h)


def _top_rows(s, n):
    rowi = lax.broadcasted_iota(jnp.int32, (n, s.shape[1]), 0)
    top = jnp.full((n, s.shape[1]), NEG_BIG, F32)
    for r in range(n):
        m = jnp.max(s, axis=0, keepdims=True)
        top = jnp.where(rowi == r, m, top)
        s = jnp.where(s == m, NEG_BIG, s)
    return top, jnp.max(s, axis=0, keepdims=True)


def _peer_prep_kernel(q_ref, keys_ref, th_ref, a_ref, s2_ref, bp_ref):
    K = PEER_TOPK
    nk = PEER_N_KEYS
    for hd in range(PEER_HEADS):
        sc = []
        for p in range(2):
            qb = q_ref[:, (2 * hd + p) * nk:(2 * hd + p + 1) * nk]
            sc.append(lax.dot_general(keys_ref[p], qb, NT_DIMS, precision=lax.Precision.HIGHEST,
                                      preferred_element_type=F32))
        s1, s2 = sc
        r1, x1 = _top_rows(s1, K)
        r2, x2 = _top_rows(s2, K)
        cand = jnp.concatenate([r1[0:1, :] + r2] + [r1[a:a + 1, :] + r2[0:8, :] for a in range(1, K)],
                               axis=0)
        ctop, c17 = _top_rows(cand, K)
        c16 = ctop[K - 1:K, :]
        c17 = jnp.maximum(c17, jnp.maximum(x1 + r2[0:1, :], r1[0:1, :] + x2))
        tau = 0.5 * (c16 + c17)
        m1, m2 = r1[0:1, :], r2[0:1, :]
        z = jnp.zeros_like(tau)
        for a in range(K):
            ra = r1[a:a + 1, :]
            z = z + jnp.exp(ra - m1) * jnp.sum(
                jnp.where(r2 >= tau - ra, jnp.exp(r2 - m2), 0.0), axis=0, keepdims=True)
        th_ref[hd] = tau - s1
        a_ref[hd] = 0.5 * jnp.exp(s1 - m1)
        s2_ref[hd] = s2
        bp_ref[hd] = jnp.exp(s2 - m2) / z


def _peer_prep(qp, sub_keys, tb=256):
    T, W = qp.shape
    tb = min(tb, T)
    nk = PEER_N_KEYS
    out = pl.BlockSpec((PEER_HEADS, nk, tb), lambda i: (0, 0, i))
    blk = tb * W * 4 + 2 * nk * nk * 4 + 4 * PEER_HEADS * nk * tb * 4
    return pl.pallas_call(
        _peer_prep_kernel, grid=(T // tb,),
        in_specs=[pl.BlockSpec((tb, W), lambda i: (i, 0)),
                  pl.BlockSpec((2, nk, sub_keys.shape[2]), lambda i: (0, 0, 0))],
        out_specs=(out,) * 4,
        out_shape=(jax.ShapeDtypeStruct((PEER_HEADS, nk, T), F32),) * 4,
        compiler_params=_cparams(("parallel",), blk, 4 * MIB), name="peer_prep")(qp, sub_keys)


def _peer_main_kernel(x_ref, u_ref, v_ref, th_ref, a_ref, s2_ref, bp_ref, o_ref, p_ref, w_ref, *,
                      n_part, n_sub):
    nk = PEER_N_KEYS
    e = pl.program_id(1)
    rows = n_sub * nk

    @pl.when(e == 0)
    def _():
        o_ref[...] = jnp.zeros_like(o_ref)

    for j in range(n_part * n_sub):
        i1 = e * (n_part * n_sub) + j
        ths = [th_ref[hd, pl.ds(i1, 1), :] for hd in range(PEER_HEADS)]
        avs = [a_ref[hd, pl.ds(i1, 1), :] for hd in range(PEER_HEADS)]
        for c in range(w_ref.shape[1] // LANES):
            ls = slice(c * LANES, (c + 1) * LANES)
            w = jnp.zeros((nk, LANES), F32)
            for hd in range(PEER_HEADS):
                w = w + jnp.where(s2_ref[hd, :, ls] >= ths[hd][:, ls], bp_ref[hd, :, ls],
                                  0.0) * avs[hd][:, ls]
            w_ref[j * nk:(j + 1) * nk, ls] = w
    for part in range(n_part):
        lo = part * rows
        hid = lax.dot_general(u_ref[lo:lo + rows, :], x_ref[...], NT_DIMS,
                              preferred_element_type=F32)
        for j in range(n_sub):
            sl = slice(lo + j * nk, lo + (j + 1) * nk)
            hj = hid[j * nk:(j + 1) * nk, :]
            act = hj * (1.0 + lax.erf(hj * (1.0 / math.sqrt(2.0))))
            p_ref[sl, :] = (act * w_ref[sl, :]).astype(BF16)
        o_ref[...] += lax.dot_general(p_ref[lo:lo + rows, :], v_ref[lo:lo + rows, :], TN_DIMS,
                                      preferred_element_type=F32)


def _peer_main(x_bf, u, v, th, a, s2, bp, tb=512, eb=1024, n_part=2):
    T, D = x_bf.shape
    E = u.shape[0]
    tb = min(tb, T)
    nk = PEER_N_KEYS
    tok = pl.BlockSpec((PEER_HEADS, nk, tb), lambda t, e: (0, 0, t))
    blk = tb * D * 2 + 2 * eb * D * 2 + 4 * PEER_HEADS * nk * tb * 4 + tb * D * 4
    return pl.pallas_call(
        functools.partial(_peer_main_kernel, n_part=n_part, n_sub=eb // (n_part * nk)),
        grid=(T // tb, E // eb),
        in_specs=[pl.BlockSpec((tb, D), lambda t, e: (t, 0)),
                  pl.BlockSpec((eb, D), lambda t, e: (e, 0)),
                  pl.BlockSpec((eb, D), lambda t, e: (e, 0)),
                  tok, tok, tok, tok],
        out_specs=pl.BlockSpec((tb, D), lambda t, e: (t, 0)),
        out_shape=jax.ShapeDtypeStruct((T, D), F32),
        scratch_shapes=[pltpu.VMEM((eb, tb), BF16), pltpu.VMEM((eb, tb), F32)],
        compiler_params=_cparams(("parallel", "arbitrary"), blk, eb * tb * 2 + 2 * eb * tb * 4),
        name="peer_main")(x_bf, u, v, th, a, s2, bp)


def _peer_ffn(x_bf, w_query, sub_keys, u, v):
    qp = _matmul(x_bf, w_query, F32, 1024, 512, "peer_query")
    th, a, s2, bp = _peer_prep(qp, sub_keys.astype(F32))
    return _peer_main(x_bf, u.astype(BF16), v.astype(BF16), th, a, s2, bp)


def _hybrid_mixer(x_bf, w_in, w_br_f, w_br_h, w_br_a, w_out, lb_f, lb_b, norm_g, B, S):
    T, D = x_bf.shape
    h = _matmul(x_bf, w_in, F32, 1024, 512, "in_proj")
    h3 = h.reshape(B, S, h.shape[1])
    y_f = _fourier_branch(h, B, S)
    o_f = _hgrn_scan(h3, OFF_Q // LANES, OFF_FF // LANES, OFF_I // LANES, lb_f, B, S, reverse=False)
    o_b = _hgrn_scan(h3, OFF_Q // LANES, OFF_FB // LANES, OFF_I // LANES, lb_b, B, S, reverse=True)
    attn = _attention_branch(h3, B, S)
    y_h, y_a = _branch_prep(o_f.reshape(T, HGRN_DK), o_b.reshape(T, HGRN_DK), h, norm_g, attn)
    w_a = jnp.pad(w_br_a.reshape(ATTN_G, ATTN_DH, D), ((0, 0), (0, LANES - ATTN_DH), (0, 0)))
    merged = _gated_merge(y_f, y_h, y_a, w_br_f.astype(BF16), w_br_h.astype(BF16),
                          w_a.reshape(ATTN_G * LANES, D).astype(BF16), h, D)
    return _matmul(merged, w_out, F32, 1024, 512, "out_proj")


def kernel(x, emb_ln_g, emb_ln_b, w_in, w_br_fourier, w_br_hgrn, w_br_attn, w_out,
           hgrn_lb_logits, hgrn_norm_g, ln_mix_g, ln_mix_b, peer_w_query, peer_sub_keys,
           peer_u, peer_v, ln_ffn_g, ln_ffn_b):
    B, S, D = x.shape
    T = B * S
    depth = w_in.shape[0]
    alpha = (2.0 * depth) ** 0.25
    lb_p = jax.nn.softmax(hgrn_lb_logits.astype(F32), axis=1)
    lb = jnp.cumsum(lb_p, axis=1) - lb_p[:, :1]
    xf, xb = _layer_norm(x.reshape(T, D), emb_ln_g, emb_ln_b)
    for l in range(depth):
        mix = _hybrid_mixer(xb, w_in[l], w_br_fourier[l], w_br_hgrn[l], w_br_attn[l], w_out[l],
                            lb[0, l], lb[1, l], hgrn_norm_g[l], B, S)
        xf, xb = _layer_norm(xf, ln_mix_g[l], ln_mix_b[l], res=mix, alpha=alpha)
        ffn = _peer_ffn(xb, peer_w_query[l], peer_sub_keys[l], peer_u[l], peer_v[l])
        xf, xb = _layer_norm(xf, ln_ffn_g[l], ln_ffn_b[l], res=ffn, alpha=alpha)
    return xf.reshape(B, S, D)
```

```python
import functools
import math

import jax
import jax.numpy as jnp
from jax import lax
from jax.experimental import pallas as pl
from jax.experimental.pallas import tpu as pltpu

F32 = jnp.float32
BF16 = jnp.bfloat16

V7X_VMEM_BYTES = 64 * 1024 * 1024
LANES = 128
MIB = 1024 * 1024

FNET_GROUPS = 4
FNET_GROUP_DIM = 128
FNET_WIDTH = FNET_GROUPS * FNET_GROUP_DIM
HGRN_HEADS = 6
HGRN_DK = HGRN_HEADS * 128
HGRN_CHUNK = 64
HGRN_NORM_EPS = 1e-6
LB_FLOOR = 1e-30
ATTN_GROUPS = ((128, 1), (512, 4), (2048, 16))
ATTN_G = 4
ATTN_DH = 64
ATTN_HEADS = ATTN_G * len(ATTN_GROUPS)
ATTN_WIDTH = ATTN_HEADS * ATTN_DH
ATTN_BLOCK = 128
ATTN_BLOCKS_PER_STEP = 4
ROPE_THETA = 500000.0
ROPE_DIM = ATTN_DH // 4
NEG_BIG = -1e30
N_BRANCHES = 3
PEER_HEADS = 8
PEER_N_KEYS = 128
PEER_TOPK = 16
LN_EPS = 1e-5

OFF_U = 0
OFF_FF = OFF_U + FNET_WIDTH
OFF_FB = OFF_FF + HGRN_DK
OFF_Q = OFF_FB + HGRN_DK
OFF_I = OFF_Q + HGRN_DK
OFF_G = OFF_I + HGRN_DK
OFF_QA = OFF_G + HGRN_DK
OFF_KA = OFF_QA + ATTN_WIDTH
OFF_VA = OFF_KA + ATTN_WIDTH
OFF_GATE = OFF_VA + ATTN_WIDTH

NT_DIMS = (((1,), (1,)), ((), ()))
TN_DIMS = (((0,), (0,)), ((), ()))


def _cparams(semantics, block_bytes, temp_bytes=0):
    need = 2 * block_bytes + temp_bytes + 4 * MIB
    limit = int(min(V7X_VMEM_BYTES - 8 * MIB, max(need, 16 * MIB)))
    return pltpu.CompilerParams(dimension_semantics=semantics, vmem_limit_bytes=limit)


def _ln_rows(v, g, b):
    mu = jnp.mean(v, axis=-1, keepdims=True)
    d = v - mu
    var = jnp.mean(d * d, axis=-1, keepdims=True)
    return d * lax.rsqrt(var + LN_EPS) * g + b


def _ln_kernel(x_ref, g_ref, b_ref, of_ref, ob_ref):
    y = _ln_rows(x_ref[...], g_ref[...], b_ref[...])
    of_ref[...] = y
    ob_ref[...] = y.astype(BF16)


def _ln_res_kernel(x_ref, y_ref, g_ref, b_ref, of_ref, ob_ref, *, alpha):
    y = _ln_rows(alpha * x_ref[...] + y_ref[...], g_ref[...], b_ref[...])
    of_ref[...] = y
    ob_ref[...] = y.astype(BF16)


def _layer_norm(x, g, b, res=None, alpha=1.0, tm=256):
    T, D = x.shape
    row = pl.BlockSpec((tm, D), lambda i: (i, 0))
    vec = pl.BlockSpec((1, D), lambda i: (0, 0))
    out_shape = (jax.ShapeDtypeStruct((T, D), F32), jax.ShapeDtypeStruct((T, D), BF16))
    g2, b2 = g.reshape(1, D).astype(F32), b.reshape(1, D).astype(F32)
    blk = tm * D * 4
    if res is None:
        return pl.pallas_call(
            _ln_kernel, grid=(T // tm,), in_specs=[row, vec, vec], out_specs=(row, row),
            out_shape=out_shape, compiler_params=_cparams(("parallel",), 3 * blk, 2 * blk),
            name="ln")(x, g2, b2)
    return pl.pallas_call(
        functools.partial(_ln_res_kernel, alpha=alpha), grid=(T // tm,),
        in_specs=[row, row, vec, vec], out_specs=(row, row), out_shape=out_shape,
        compiler_params=_cparams(("parallel",), 4 * blk, 2 * blk), name="ln_res")(x, res, g2, b2)


def _mm_kernel(a_ref, b_ref, o_ref, wb_ref):
    @pl.when(pl.program_id(1) == 0)
    def _():
        wb_ref[...] = b_ref[...].astype(BF16)

    o_ref[...] = jnp.dot(a_ref[...], wb_ref[...], preferred_element_type=F32).astype(o_ref.dtype)


def _matmul(a, b, out_dtype, tm, tn, name):
    M, K = a.shape
    N = b.shape[1]
    tm, tn = min(tm, M), min(tn, N)
    blk = tm * K * 2 + K * tn * 4 + tm * tn * jnp.dtype(out_dtype).itemsize
    return pl.pallas_call(
        _mm_kernel, grid=(N // tn, M // tm),
        in_specs=[pl.BlockSpec((tm, K), lambda j, i: (i, 0)),
                  pl.BlockSpec((K, tn), lambda j, i: (0, j))],
        out_specs=pl.BlockSpec((tm, tn), lambda j, i: (i, j)),
        out_shape=jax.ShapeDtypeStruct((M, N), out_dtype),
        scratch_shapes=[pltpu.VMEM((K, tn), BF16)],
        compiler_params=_cparams(("parallel", "arbitrary"), blk, tm * tn * 4 + K * tn * 2),
        name=name)(a, b)


DFT_SPLIT = 64


def _dft_angles(rows, n, period):
    k = jnp.arange(n, dtype=jnp.int32)
    r = jnp.arange(rows, dtype=jnp.int32)
    return ((r[:, None] * k[None, :]) % period).astype(F32) * (2.0 * math.pi / period)


def _dft_tables(n, scale):
    if n <= DFT_SPLIT * 8:
        ang = _dft_angles(n, n, n)
        c, s = jnp.cos(ang), jnp.sin(ang)
    else:
        hi = _dft_angles(n // DFT_SPLIT, n, n // DFT_SPLIT)
        lo = _dft_angles(DFT_SPLIT, n, n)
        ch, sh = jnp.cos(hi)[:, None, :], jnp.sin(hi)[:, None, :]
        cl, sl = jnp.cos(lo)[None, :, :], jnp.sin(lo)[None, :, :]
        c = (ch * cl - sh * sl).reshape(n, n)
        s = (sh * cl + ch * sl).reshape(n, n)
    return (c * scale).astype(BF16), (s * scale).astype(BF16)


def _dft_chan_kernel(u_ref, cs_ref, p_ref, q_ref):
    gd = FNET_GROUP_DIM
    for g in range(FNET_GROUPS):
        ug = u_ref[:, g * gd:(g + 1) * gd].astype(BF16)
        r = jnp.dot(ug, cs_ref[...], preferred_element_type=F32)
        p_ref[:, g * gd:(g + 1) * gd] = r[:, :gd].astype(BF16)
        q_ref[:, g * gd:(g + 1) * gd] = r[:, gd:].astype(BF16)


def _dft_seq_kernel(c_ref, s_ref, p_ref, q_ref, o_ref):
    acc = jnp.dot(c_ref[...], p_ref[...], preferred_element_type=F32)
    acc = acc - jnp.dot(s_ref[...], q_ref[...], preferred_element_type=F32)
    o_ref[...] = acc.astype(o_ref.dtype)


def _fourier_branch(h, B, S, tm=512):
    T = B * S
    gd = FNET_GROUP_DIM
    cc, sc = _dft_tables(gd, gd ** -0.5)
    cs = jnp.concatenate([cc, sc], axis=1)
    tm1 = min(tm, T)
    p, q = pl.pallas_call(
        _dft_chan_kernel, grid=(T // tm1,),
        in_specs=[pl.BlockSpec((tm1, FNET_WIDTH), lambda i: (i, OFF_U // FNET_WIDTH)),
                  pl.BlockSpec((gd, 2 * gd), lambda i: (0, 0))],
        out_specs=(pl.BlockSpec((tm1, FNET_WIDTH), lambda i: (i, 0)),) * 2,
        out_shape=(jax.ShapeDtypeStruct((T, FNET_WIDTH), BF16),) * 2,
        compiler_params=_cparams(("parallel",), tm1 * FNET_WIDTH * 8), name="dft_chan")(h, cs)
    cseq, sseq = _dft_tables(S, S ** -0.5)
    tm2 = min(tm, S)
    blk = 2 * tm2 * S * 2 + 2 * S * FNET_WIDTH * 2 + tm2 * FNET_WIDTH * 2
    y = pl.pallas_call(
        _dft_seq_kernel, grid=(S // tm2, B),
        in_specs=[pl.BlockSpec((tm2, S), lambda i, b: (i, 0)),
                  pl.BlockSpec((tm2, S), lambda i, b: (i, 0)),
                  pl.BlockSpec((None, S, FNET_WIDTH), lambda i, b: (b, 0, 0)),
                  pl.BlockSpec((None, S, FNET_WIDTH), lambda i, b: (b, 0, 0))],
        out_specs=pl.BlockSpec((None, tm2, FNET_WIDTH), lambda i, b: (b, i, 0)),
        out_shape=jax.ShapeDtypeStruct((B, S, FNET_WIDTH), BF16),
        compiler_params=_cparams(("parallel", "parallel"), blk, tm2 * FNET_WIDTH * 8),
        name="dft_seq")(cseq, sseq, p.reshape(B, S, FNET_WIDTH), q.reshape(B, S, FNET_WIDTH))
    return y.reshape(T, FNET_WIDTH)


def _hgrn_pivots(b, row, reverse):
    C = HGRN_CHUNK
    pivots = []
    for c in (32, 16, 8, 4):
        n = C // (2 * c)
        at = c if reverse else c - 1
        piv = b.reshape(n, 2 * c, LANES)[:, at:at + 1, :]
        pivots.append(jnp.broadcast_to(piv, (n, 2 * c, LANES)).reshape(C, LANES))
    r4 = row & 3
    up1, up2 = pltpu.roll(b, C - 1, 0), pltpu.roll(b, C - 2, 0)
    dn1, dn2 = pltpu.roll(b, 1, 0), pltpu.roll(b, 2, 0)
    if reverse:
        pivots.append(jnp.where(r4 == 0, up2, jnp.where(r4 == 1, up1, jnp.where(r4 == 2, b, dn1))))
        pivots.append(jnp.where((row & 1) == 0, up1, b))
    else:
        pivots.append(jnp.where(r4 == 0, up1, jnp.where(r4 == 1, b, jnp.where(r4 == 2, dn1, dn2))))
        pivots.append(jnp.where((row & 1) == 0, b, dn1))
    return pivots


HGRN_HEADS_PER_STEP = 2


def _hgrn_kernel(q_ref, z_ref, v_ref, lb_ref, o_ref, st_ref, *, n_chunks, reverse):
    C = HGRN_CHUNK

    @pl.when(pl.program_id(2) == 0)
    def _():
        st_ref[...] = jnp.zeros_like(st_ref)

    row = lax.broadcasted_iota(jnp.int32, (C, LANES), 0)
    tt = lax.broadcasted_iota(jnp.int32, (C, C), 0)
    ss = lax.broadcasted_iota(jnp.int32, (C, C), 1)
    src, dst = (tt, ss) if reverse else (ss, tt)
    level_masks = []
    for sh in (5, 4, 3, 2, 1, 0):
        sb, db = src >> sh, dst >> sh
        level_masks.append(sb == jnp.where((db & 1) == 1, db - 1, -1))
    diag_mask = tt == ss
    last = 0 if reverse else C - 1

    def head_chunk(r0, hh):
        ls = slice(hh * LANES, (hh + 1) * LANES)
        lb = lb_ref[:, ls]
        log_lb = jnp.log(jnp.maximum(lb, LB_FLOOR))
        q = q_ref[pl.ds(r0, C), ls]
        z = z_ref[pl.ds(r0, C), ls]
        v = v_ref[pl.ds(r0, C), ls].astype(BF16)
        log_sig = jnp.minimum(z, 0.0) - jnp.log1p(jnp.exp(-jnp.abs(z)))
        t = jnp.log1p(-lb) + log_sig
        logf = jnp.maximum(log_lb, t) + jnp.log1p(jnp.exp(-jnp.abs(log_lb - t)))
        k = (1.0 - lb) / (1.0 + jnp.exp(z))
        b = logf
        for s in (1, 2, 4, 8, 16, 32):
            if reverse:
                b = b + jnp.where(row < C - s, pltpu.roll(b, C - s, 0), 0.0)
            else:
                b = b + jnp.where(row >= s, pltpu.roll(b, s, 0), 0.0)
        qb, kb = q.astype(BF16), k.astype(BF16)
        a = jnp.where(diag_mask, lax.dot_general(qb, kb, NT_DIMS, preferred_element_type=F32), 0.0)
        for piv, mask in zip(_hgrn_pivots(b, row, reverse), level_masks):
            e = jnp.exp(-jnp.abs(b - piv))
            al = lax.dot_general((q * e).astype(BF16), (k * e).astype(BF16), NT_DIMS,
                                 preferred_element_type=F32)
            a = a + jnp.where(mask, al, 0.0)
        st = st_ref[hh]
        b_last = b[last:last + 1, :]
        o = jnp.dot(a.astype(BF16), v, preferred_element_type=F32)
        o = o + lax.dot_general((q * jnp.exp(b)).astype(BF16), st.astype(BF16), NT_DIMS,
                                preferred_element_type=F32)
        o_ref[pl.ds(r0, C), ls] = o
        k_tail = (k * jnp.exp(b_last - b)).astype(BF16)
        st_ref[hh] = st * jnp.exp(b_last) + lax.dot_general(v, k_tail, TN_DIMS,
                                                            preferred_element_type=F32)

    def chunk(i, carry):
        ci = n_chunks - 1 - i if reverse else i
        r0 = pl.multiple_of(ci * C, C)
        for hh in range(HGRN_HEADS_PER_STEP):
            head_chunk(r0, hh)
        return carry

    lax.fori_loop(0, n_chunks, chunk, 0, unroll=2)


def _hgrn_scan(src, off_q, off_z, off_v, lb, B, S, reverse, rb=512):
    rb = min(rb, S)
    nblk = S // rb
    nh = HGRN_HEADS_PER_STEP
    w = nh * LANES
    blk_idx = (lambda c: nblk - 1 - c) if reverse else (lambda c: c)
    spec = lambda off: pl.BlockSpec((None, rb, w), lambda b, h, c: (b, blk_idx(c), off // w + h))
    return pl.pallas_call(
        functools.partial(_hgrn_kernel, n_chunks=rb // HGRN_CHUNK, reverse=reverse),
        grid=(B, HGRN_HEADS // nh, nblk),
        in_specs=[spec(off_q), spec(off_z), spec(off_v),
                  pl.BlockSpec((None, 1, w), lambda b, h, c: (h, 0, 0))],
        out_specs=pl.BlockSpec((None, rb, w), lambda b, h, c: (b, blk_idx(c), h)),
        out_shape=jax.ShapeDtypeStruct((B, S, HGRN_DK), F32),
        scratch_shapes=[pltpu.VMEM((nh, LANES, LANES), F32)],
        compiler_params=_cparams(("parallel", "parallel", "arbitrary"), 4 * rb * w * 4, 2 * MIB),
        name="hgrn_scan_rev" if reverse else "hgrn_scan")(
            src, src, src, lb.reshape(HGRN_HEADS // nh, 1, w).astype(F32))


def _rope(t, cos, sin, perm):
    rot = jnp.dot(t.astype(BF16), perm, preferred_element_type=F32)
    return t * cos + rot * sin


def _attn_kernel(q_ref, k_ref, v_ref, cos_ref, sin_ref, o_ref, *, L, radius, nseq, nblk):
    blk = ATTN_BLOCK
    width = blk + 2 * radius
    pr = lax.broadcasted_iota(jnp.int32, (ATTN_DH, ATTN_DH), 0)
    pc = lax.broadcasted_iota(jnp.int32, (ATTN_DH, ATTN_DH), 1)
    half = ROPE_DIM // 2
    src = jnp.where(pc < half, pc + half, jnp.where(pc < ROPE_DIM, pc - half, -1))
    perm = jnp.where(pr == src, 1.0, 0.0).astype(BF16)
    row = lax.broadcasted_iota(jnp.int32, (blk, width), 0)
    col = lax.broadcasted_iota(jnp.int32, (blk, width), 1)
    lane = lax.broadcasted_iota(jnp.int32, (blk, LANES), 1)
    for n in range(nseq):
        for j in range(nblk):
            q0 = pl.multiple_of((pl.program_id(1) * nblk + j) * blk, blk)
            start = pl.multiple_of(jnp.clip(q0 - radius, 0, L - width), 8)
            qr = _rope(q_ref[n, j * blk:(j + 1) * blk, :], cos_ref[pl.ds(q0, blk), :],
                       sin_ref[pl.ds(q0, blk), :], perm)
            kr = _rope(k_ref[n, pl.ds(start, width), :], cos_ref[pl.ds(start, width), :],
                       sin_ref[pl.ds(start, width), :], perm)
            s = lax.dot_general(qr.astype(BF16), kr.astype(BF16), NT_DIMS,
                                preferred_element_type=F32)
            s = s * (1.0 / math.sqrt(ATTN_DH))
            s = jnp.where(jnp.abs(col - row + (start - q0)) <= radius, s, NEG_BIG)
            m = jnp.max(s, axis=-1, keepdims=True)
            p = jnp.exp(s - m)
            den = jnp.sum(p, axis=-1, keepdims=True)
            o = jnp.dot(p.astype(BF16), v_ref[n, pl.ds(start, width), :].astype(BF16),
                        preferred_element_type=F32)
            lse = m + jnp.log(den)
            o_ref[n, j * blk:(j + 1) * blk, :] = jnp.where(lane < ATTN_DH, o / den, lse)


def _band_attention(q, k, v, cos, sin, dil, radius):
    N, L, _ = q.shape
    nblk = min(ATTN_BLOCKS_PER_STEP, L // ATTN_BLOCK)
    nseq = ATTN_BLOCKS_PER_STEP // nblk
    rows = nblk * ATTN_BLOCK
    res = lambda n, i: ((n * nseq // ATTN_G) % dil, 0, 0)
    blk = nseq * (rows * LANES * 4 * 2 + 2 * L * LANES * 4) + 2 * L * LANES * 4
    return pl.pallas_call(
        functools.partial(_attn_kernel, L=L, radius=radius, nseq=nseq, nblk=nblk),
        grid=(N // nseq, L // rows),
        in_specs=[pl.BlockSpec((nseq, rows, ATTN_DH), lambda n, i: (n, i, 0)),
                  pl.BlockSpec((nseq, L, ATTN_DH), lambda n, i: (n, 0, 0)),
                  pl.BlockSpec((nseq, L, LANES), lambda n, i: (n, 0, 0)),
                  pl.BlockSpec((None, L, ATTN_DH), res),
                  pl.BlockSpec((None, L, ATTN_DH), res)],
        out_specs=pl.BlockSpec((nseq, rows, LANES), lambda n, i: (n, i, 0)),
        out_shape=jax.ShapeDtypeStruct((N, L, LANES), F32),
        compiler_params=_cparams(("parallel", "parallel"), blk, 4 * MIB),
        name=f"band_attn_d{dil}")(q, k, v, cos, sin)


def _attention_branch(h3, B, S):
    pos = jnp.arange(S, dtype=F32)
    inv_freq = ROPE_THETA ** (-jnp.arange(0, ROPE_DIM, 2, dtype=F32) / ROPE_DIM)
    ang = pos[:, None] * inv_freq[None, :]
    cos, sin = jnp.cos(ang), jnp.sin(ang)
    rest = ATTN_DH - ROPE_DIM
    cos_f = jnp.concatenate([cos, cos, jnp.ones((S, rest), F32)], axis=1)
    sin_f = jnp.concatenate([-sin, sin, jnp.zeros((S, rest), F32)], axis=1)
    outs = []
    for gi, (window, dil) in enumerate(ATTN_GROUPS):
        radius = window // (2 * dil)
        L = S // dil

        def to_sub(off):
            t = h3[:, :, off + gi * ATTN_G * ATTN_DH: off + (gi + 1) * ATTN_G * ATTN_DH]
            t = t.reshape(B, L, dil, ATTN_G, ATTN_DH).transpose(0, 2, 3, 1, 4)
            return t.reshape(B * dil * ATTN_G, L, ATTN_DH)

        tab = lambda t: t.reshape(L, dil, ATTN_DH).transpose(1, 0, 2)
        v = jnp.pad(to_sub(OFF_VA), ((0, 0), (0, 0), (0, LANES - ATTN_DH)))
        o = _band_attention(to_sub(OFF_QA), to_sub(OFF_KA), v, tab(cos_f), tab(sin_f), dil, radius)
        o = o.reshape(B, dil, ATTN_G, L, LANES).transpose(0, 3, 1, 2, 4)
        outs.append(o.reshape(B * S, ATTN_G * LANES))
    return outs


def _sigmoid(x):
    return 1.0 / (1.0 + jnp.exp(-x))


def _prep_kernel(of_ref, ob_ref, g0_ref, g1_ref, g2_ref, ng_ref, a0_ref, a1_ref, a2_ref,
                 yh_ref, ya_ref):
    g_refs = (g0_ref, g1_ref, g2_ref)
    for hd in range(HGRN_HEADS):
        sl = slice(hd * LANES, (hd + 1) * LANES)
        o = of_ref[:, sl] + ob_ref[:, sl]
        o = o * lax.rsqrt(jnp.mean(o * o, axis=-1, keepdims=True) + HGRN_NORM_EPS)
        g = g_refs[hd // 2][:, (hd % 2) * LANES:(hd % 2 + 1) * LANES]
        yh_ref[:, sl] = (o * ng_ref[:, sl] * (g * _sigmoid(g))).astype(BF16)
    lane = lax.broadcasted_iota(jnp.int32, (of_ref.shape[0], LANES), 1)
    for hd in range(ATTN_G):
        sl = slice(hd * LANES, (hd + 1) * LANES)
        xs = [r[:, sl] for r in (a0_ref, a1_ref, a2_ref)]
        mx = jnp.maximum(jnp.maximum(xs[0], xs[1]), xs[2])
        es = [jnp.exp(x - mx) for x in xs]
        inv = 1.0 / (es[0] + es[1] + es[2])
        out = sum(pltpu.roll(e * inv, ATTN_DH, 1) * x for e, x in zip(es, xs))
        ya_ref[:, sl] = jnp.where(lane < ATTN_DH, out, 0.0).astype(BF16)


def _branch_prep(o_f, o_b, h, norm_g, attn, tm=256):
    T = o_f.shape[0]
    wa = ATTN_G * LANES
    rows = lambda w: pl.BlockSpec((tm, w), lambda i: (i, 0))
    blk = tm * (3 * HGRN_DK + 3 * wa) * 4 + tm * (HGRN_DK + wa) * 2
    gw = 2 * LANES
    gate = lambda j: pl.BlockSpec((tm, gw), lambda i: (i, OFF_G // gw + j))
    return pl.pallas_call(
        _prep_kernel, grid=(T // tm,),
        in_specs=[rows(HGRN_DK), rows(HGRN_DK), gate(0), gate(1), gate(2),
                  pl.BlockSpec((1, HGRN_DK), lambda i: (0, 0)), rows(wa), rows(wa), rows(wa)],
        out_specs=(rows(HGRN_DK), rows(wa)),
        out_shape=(jax.ShapeDtypeStruct((T, HGRN_DK), BF16), jax.ShapeDtypeStruct((T, wa), BF16)),
        compiler_params=_cparams(("parallel",), blk, 2 * MIB), name="branch_prep")(
            o_f, o_b, h, h, h, norm_g.reshape(1, HGRN_DK).astype(F32), *attn)


def _merge_kernel(yf_ref, yh_ref, ya_ref, wf_ref, wh_ref, wa_ref, g0_ref, g1_ref, g2_ref, o_ref):
    dot = lambda y, w: jnp.dot(y[...], w[...], preferred_element_type=F32)
    acc = _sigmoid(g0_ref[...]) * dot(yf_ref, wf_ref)
    acc = acc + _sigmoid(g1_ref[...]) * dot(yh_ref, wh_ref)
    acc = acc + _sigmoid(g2_ref[...]) * dot(ya_ref, wa_ref)
    o_ref[...] = acc.astype(o_ref.dtype)


def _gated_merge(y_f, y_h, y_a, w_f, w_h, w_a, h, D, tm=512, tn=512):
    T = y_f.shape[0]
    tm = min(tm, T)
    kf, kh, ka = y_f.shape[1], y_h.shape[1], y_a.shape[1]
    rows = lambda w: pl.BlockSpec((tm, w), lambda j, i: (i, 0))
    cols = lambda kk: pl.BlockSpec((kk, tn), lambda j, i: (0, j))
    gate = lambda br: pl.BlockSpec((tm, tn), lambda j, i: (i, (OFF_GATE + br * D) // tn + j))
    blk = tm * (kf + kh + ka) * 2 + (kf + kh + ka) * tn * 2 + 3 * tm * tn * 4 + tm * tn * 2
    return pl.pallas_call(
        _merge_kernel, grid=(D // tn, T // tm),
        in_specs=[rows(kf), rows(kh), rows(ka), cols(kf), cols(kh), cols(ka),
                  gate(0), gate(1), gate(2)],
        out_specs=pl.BlockSpec((tm, tn), lambda j, i: (i, j)),
        out_shape=jax.ShapeDtypeStruct((T, D), BF16),
        compiler_params=_cparams(("parallel", "parallel"), blk, 4 * tm * tn * 4),
        name="gated_merge")(y_f, y_h, y_a, w_f, w_h, w_a, h, h, h)


def _top_rows(s, n, with_rank=False):
    rowi = lax.broadcasted_iota(jnp.int32, (n, s.shape[1]), 0)
    top = jnp.full((n, s.shape[1]), NEG_BIG, F32)
    rank = jnp.full(s.shape, float(n), F32)
    for r in range(n):
        m = jnp.max(s, axis=0, keepdims=True)
        top = jnp.where(rowi == r, m, top)
        hit = s == m
        if with_rank:
            rank = jnp.where(hit, float(r), rank)
        s = jnp.where(hit, NEG_BIG, s)
    return top, jnp.max(s, axis=0, keepdims=True), rank


def _peer_prep_kernel(q_ref, keys_ref, n_ref, a_ref, rk_ref, bp_ref):
    K = PEER_TOPK
    nk = PEER_N_KEYS
    for hd in range(PEER_HEADS):
        sc = []
        for p in range(2):
            qb = q_ref[:, (2 * hd + p) * nk:(2 * hd + p + 1) * nk]
            sc.append(lax.dot_general(keys_ref[p], qb, NT_DIMS, precision=lax.Precision.HIGHEST,
                                      preferred_element_type=F32))
        s1, s2 = sc
        r1, x1, _ = _top_rows(s1, K)
        r2, x2, rank2 = _top_rows(s2, K, with_rank=True)
        cand = jnp.concatenate([r1[0:1, :] + r2] + [r1[a:a + 1, :] + r2[0:8, :] for a in range(1, K)],
                               axis=0)
        ctop, c17, _ = _top_rows(cand, K)
        c16 = ctop[K - 1:K, :]
        c17 = jnp.maximum(c17, jnp.maximum(x1 + r2[0:1, :], r1[0:1, :] + x2))
        tau = 0.5 * (c16 + c17)
        m1, m2 = r1[0:1, :], r2[0:1, :]
        z = jnp.zeros_like(tau)
        for a in range(K):
            ra = r1[a:a + 1, :]
            z = z + jnp.exp(ra - m1) * jnp.sum(
                jnp.where(r2 >= tau - ra, jnp.exp(r2 - m2), 0.0), axis=0, keepdims=True)
        bound = tau - s1
        count = jnp.zeros_like(s1)
        for b in range(K):
            count = count + jnp.where(r2[b:b + 1, :] >= bound, 1.0, 0.0)
        n_ref[hd] = count
        a_ref[hd] = 0.5 * jnp.exp(s1 - m1)
        rk_ref[hd] = rank2.astype(BF16)
        bp_ref[hd] = (jnp.exp(s2 - m2) / z).astype(BF16)


def _peer_prep(qp, sub_keys, tb=256):
    T, W = qp.shape
    tb = min(tb, T)
    nk = PEER_N_KEYS
    out = pl.BlockSpec((PEER_HEADS, nk, tb), lambda i: (0, 0, i))
    blk = tb * W * 4 + 2 * nk * nk * 4 + PEER_HEADS * nk * tb * (4 + 4 + 2 + 2)
    shape = lambda dt: jax.ShapeDtypeStruct((PEER_HEADS, nk, T), dt)
    return pl.pallas_call(
        _peer_prep_kernel, grid=(T // tb,),
        in_specs=[pl.BlockSpec((tb, W), lambda i: (i, 0)),
                  pl.BlockSpec((2, nk, sub_keys.shape[2]), lambda i: (0, 0, 0))],
        out_specs=(out,) * 4,
        out_shape=(shape(F32), shape(F32), shape(BF16), shape(BF16)),
        compiler_params=_cparams(("parallel",), blk, 4 * MIB), name="peer_prep")(qp, sub_keys)


def _peer_main_kernel(x_ref, u_ref, v_ref, n_ref, a_ref, rk_ref, bp_ref, o_ref, p_ref, *,
                      n_part, n_sub):
    nk = PEER_N_KEYS
    e = pl.program_id(1)
    rows = n_sub * nk
    tb = p_ref.shape[1]

    @pl.when(e == 0)
    def _():
        o_ref[...] = jnp.zeros_like(o_ref)

    for part in range(n_part):
        lo = part * rows
        hid = lax.dot_general(u_ref[lo:lo + rows, :], x_ref[...], NT_DIMS,
                              preferred_element_type=F32)
        for j in range(n_sub):
            i1 = (e * n_part + part) * n_sub + j
            ns = [n_ref[hd, pl.ds(i1, 1), :].astype(BF16) for hd in range(PEER_HEADS)]
            avs = [a_ref[hd, pl.ds(i1, 1), :].astype(BF16) for hd in range(PEER_HEADS)]
            for c in range(tb // LANES):
                ls = slice(c * LANES, (c + 1) * LANES)
                w = jnp.zeros((nk, LANES), BF16)
                for hd in range(PEER_HEADS):
                    w = w + jnp.where(rk_ref[hd, :, ls] < ns[hd][:, ls], bp_ref[hd, :, ls],
                                      jnp.zeros((), BF16)) * avs[hd][:, ls]
                hj = hid[j * nk:(j + 1) * nk, ls]
                act = hj * (1.0 + lax.erf(hj * (1.0 / math.sqrt(2.0))))
                p_ref[lo + j * nk:lo + (j + 1) * nk, ls] = act.astype(BF16) * w
        o_ref[...] += lax.dot_general(p_ref[lo:lo + rows, :], v_ref[lo:lo + rows, :], TN_DIMS,
                                      preferred_element_type=F32)


def _peer_main(x_bf, u, v, n, a, rk, bp, tb=512, eb=1024, n_part=2):
    T, D = x_bf.shape
    E = u.shape[0]
    tb = min(tb, T)
    nk = PEER_N_KEYS
    tok = pl.BlockSpec((PEER_HEADS, nk, tb), lambda t, e: (0, 0, t))
    blk = tb * D * 2 + 2 * eb * D * 2 + PEER_HEADS * nk * tb * (4 + 4 + 2 + 2) + tb * D * 4
    return pl.pallas_call(
        functools.partial(_peer_main_kernel, n_part=n_part, n_sub=eb // (n_part * nk)),
        grid=(T // tb, E // eb),
        in_specs=[pl.BlockSpec((tb, D), lambda t, e: (t, 0)),
                  pl.BlockSpec((eb, D), lambda t, e: (e, 0)),
                  pl.BlockSpec((eb, D), lambda t, e: (e, 0)),
                  tok, tok, tok, tok],
        out_specs=pl.BlockSpec((tb, D), lambda t, e: (t, 0)),
        out_shape=jax.ShapeDtypeStruct((T, D), F32),
        scratch_shapes=[pltpu.VMEM((eb, tb), BF16)],
        compiler_params=_cparams(("parallel", "arbitrary"), blk, eb * tb * 2 + 2 * eb * tb * 4),
        name="peer_main")(x_bf, u, v, n, a, rk, bp)


def _peer_ffn(x_bf, w_query, sub_keys, u, v):
    qp = _matmul(x_bf, w_query, F32, 1024, 512, "peer_query")
    n, a, rk, bp = _peer_prep(qp, sub_keys.astype(F32))
    return _peer_main(x_bf, u.astype(BF16), v.astype(BF16), n, a, rk, bp)


def _hybrid_mixer(x_bf, w_in, w_br_f, w_br_h, w_br_a, w_out, lb_f, lb_b, norm_g, B, S):
    T, D = x_bf.shape
    h = _matmul(x_bf, w_in, F32, 1024, 512, "in_proj")
    h3 = h.reshape(B, S, h.shape[1])
    y_f = _fourier_branch(h, B, S)
    o_f = _hgrn_scan(h3, OFF_Q, OFF_FF, OFF_I, lb_f, B, S, reverse=False)
    o_b = _hgrn_scan(h3, OFF_Q, OFF_FB, OFF_I, lb_b, B, S, reverse=True)
    attn = _attention_branch(h3, B, S)
    y_h, y_a = _branch_prep(o_f.reshape(T, HGRN_DK), o_b.reshape(T, HGRN_DK), h, norm_g, attn)
    w_a = jnp.pad(w_br_a.reshape(ATTN_G, ATTN_DH, D), ((0, 0), (0, LANES - ATTN_DH), (0, 0)))
    merged = _gated_merge(y_f, y_h, y_a, w_br_f.astype(BF16), w_br_h.astype(BF16),
                          w_a.reshape(ATTN_G * LANES, D).astype(BF16), h, D)
    return _matmul(merged, w_out, F32, 1024, 512, "out_proj")


def kernel(x, emb_ln_g, emb_ln_b, w_in, w_br_fourier, w_br_hgrn, w_br_attn, w_out,
           hgrn_lb_logits, hgrn_norm_g, ln_mix_g, ln_mix_b, peer_w_query, peer_sub_keys,
           peer_u, peer_v, ln_ffn_g, ln_ffn_b):
    B, S, D = x.shape
    T = B * S
    depth = w_in.shape[0]
    alpha = (2.0 * depth) ** 0.25
    lb_p = jax.nn.softmax(hgrn_lb_logits.astype(F32), axis=1)
    lb = jnp.cumsum(lb_p, axis=1) - lb_p[:, :1]
    xf, xb = _layer_norm(x.reshape(T, D), emb_ln_g, emb_ln_b)
    for l in range(depth):
        mix = _hybrid_mixer(xb, w_in[l], w_br_fourier[l], w_br_hgrn[l], w_br_attn[l], w_out[l],
                            lb[0, l], lb[1, l], hgrn_norm_g[l], B, S)
        xf, xb = _layer_norm(xf, ln_mix_g[l], ln_mix_b[l], res=mix, alpha=alpha)
        ffn = _peer_ffn(xb, peer_w_query[l], peer_sub_keys[l], peer_u[l], peer_v[l])
        xf, xb = _layer_norm(xf, ln_ffn_g[l], ln_ffn_b[l], res=ffn, alpha=alpha)
    return xf.reshape(B, S, D)
```

```python
import functools
import math

import jax
import jax.numpy as jnp
from jax import lax
from jax.experimental import pallas as pl
from jax.experimental.pallas import tpu as pltpu

F32 = jnp.float32
BF16 = jnp.bfloat16

V7X_VMEM_BYTES = 64 * 1024 * 1024
LANES = 128
MIB = 1024 * 1024

FNET_GROUPS = 4
FNET_GROUP_DIM = 128
FNET_WIDTH = FNET_GROUPS * FNET_GROUP_DIM
HGRN_HEADS = 6
HGRN_DK = HGRN_HEADS * 128
HGRN_CHUNK = 64
HGRN_NORM_EPS = 1e-6
LB_FLOOR = 1e-30
ATTN_GROUPS = ((128, 1), (512, 4), (2048, 16))
ATTN_G = 4
ATTN_DH = 64
ATTN_HEADS = ATTN_G * len(ATTN_GROUPS)
ATTN_WIDTH = ATTN_HEADS * ATTN_DH
ATTN_BLOCK = 128
ROPE_THETA = 500000.0
ROPE_DIM = ATTN_DH // 4
NEG_BIG = -1e30
N_BRANCHES = 3
PEER_HEADS = 8
PEER_N_KEYS = 128
PEER_TOPK = 16
LN_EPS = 1e-5

OFF_U = 0
OFF_FF = OFF_U + FNET_WIDTH
OFF_FB = OFF_FF + HGRN_DK
OFF_Q = OFF_FB + HGRN_DK
OFF_I = OFF_Q + HGRN_DK
OFF_G = OFF_I + HGRN_DK
OFF_QA = OFF_G + HGRN_DK
OFF_KA = OFF_QA + ATTN_WIDTH
OFF_VA = OFF_KA + ATTN_WIDTH
OFF_GATE = OFF_VA + ATTN_WIDTH

NT_DIMS = (((1,), (1,)), ((), ()))
TN_DIMS = (((0,), (0,)), ((), ()))


def _cparams(semantics, block_bytes, temp_bytes=0):
    need = 2 * block_bytes + temp_bytes + 4 * MIB
    limit = int(min(V7X_VMEM_BYTES - 8 * MIB, max(need, 16 * MIB)))
    return pltpu.CompilerParams(dimension_semantics=semantics, vmem_limit_bytes=limit)


def _ln_rows(v, g, b):
    mu = jnp.mean(v, axis=-1, keepdims=True)
    d = v - mu
    var = jnp.mean(d * d, axis=-1, keepdims=True)
    return d * lax.rsqrt(var + LN_EPS) * g + b


def _ln_kernel(x_ref, g_ref, b_ref, of_ref, ob_ref):
    y = _ln_rows(x_ref[...], g_ref[...], b_ref[...])
    of_ref[...] = y
    ob_ref[...] = y.astype(BF16)


def _ln_res_kernel(x_ref, y_ref, g_ref, b_ref, of_ref, ob_ref, *, alpha):
    y = _ln_rows(alpha * x_ref[...] + y_ref[...], g_ref[...], b_ref[...])
    of_ref[...] = y
    ob_ref[...] = y.astype(BF16)


def _layer_norm(x, g, b, res=None, alpha=1.0, tm=256):
    T, D = x.shape
    row = pl.BlockSpec((tm, D), lambda i: (i, 0))
    vec = pl.BlockSpec((1, D), lambda i: (0, 0))
    out_shape = (jax.ShapeDtypeStruct((T, D), F32), jax.ShapeDtypeStruct((T, D), BF16))
    g2, b2 = g.reshape(1, D).astype(F32), b.reshape(1, D).astype(F32)
    blk = tm * D * 4
    if res is None:
        return pl.pallas_call(
            _ln_kernel, grid=(T // tm,), in_specs=[row, vec, vec], out_specs=(row, row),
            out_shape=out_shape, compiler_params=_cparams(("parallel",), 3 * blk, 2 * blk),
            name="ln")(x, g2, b2)
    return pl.pallas_call(
        functools.partial(_ln_res_kernel, alpha=alpha), grid=(T // tm,),
        in_specs=[row, row, vec, vec], out_specs=(row, row), out_shape=out_shape,
        compiler_params=_cparams(("parallel",), 4 * blk, 2 * blk), name="ln_res")(x, res, g2, b2)


def _mm_kernel(a_ref, b_ref, o_ref, wb_ref):
    @pl.when(pl.program_id(1) == 0)
    def _():
        wb_ref[...] = b_ref[...].astype(BF16)

    o_ref[...] = jnp.dot(a_ref[...], wb_ref[...], preferred_element_type=F32).astype(o_ref.dtype)


def _matmul(a, b, out_dtype, tm, tn, name):
    M, K = a.shape
    N = b.shape[1]
    tm, tn = min(tm, M), min(tn, N)
    blk = tm * K * 2 + K * tn * 4 + tm * tn * jnp.dtype(out_dtype).itemsize
    return pl.pallas_call(
        _mm_kernel, grid=(N // tn, M // tm),
        in_specs=[pl.BlockSpec((tm, K), lambda j, i: (i, 0)),
                  pl.BlockSpec((K, tn), lambda j, i: (0, j))],
        out_specs=pl.BlockSpec((tm, tn), lambda j, i: (i, j)),
        out_shape=jax.ShapeDtypeStruct((M, N), out_dtype),
        scratch_shapes=[pltpu.VMEM((K, tn), BF16)],
        compiler_params=_cparams(("parallel", "arbitrary"), blk, tm * tn * 4 + K * tn * 2),
        name=name)(a, b)


DFT_SPLIT = 64


def _dft_angles(rows, n, period):
    k = jnp.arange(n, dtype=jnp.int32)
    r = jnp.arange(rows, dtype=jnp.int32)
    return ((r[:, None] * k[None, :]) % period).astype(F32) * (2.0 * math.pi / period)


def _dft_tables(n, scale):
    if n <= DFT_SPLIT * 8:
        ang = _dft_angles(n, n, n)
        c, s = jnp.cos(ang), jnp.sin(ang)
    else:
        hi = _dft_angles(n // DFT_SPLIT, n, n // DFT_SPLIT)
        lo = _dft_angles(DFT_SPLIT, n, n)
        ch, sh = jnp.cos(hi)[:, None, :], jnp.sin(hi)[:, None, :]
        cl, sl = jnp.cos(lo)[None, :, :], jnp.sin(lo)[None, :, :]
        c = (ch * cl - sh * sl).reshape(n, n)
        s = (sh * cl + ch * sl).reshape(n, n)
    return (c * scale).astype(BF16), (s * scale).astype(BF16)


def _dft_chan_kernel(u_ref, cs_ref, p_ref, q_ref):
    gd = FNET_GROUP_DIM
    for g in range(FNET_GROUPS):
        ug = u_ref[:, g * gd:(g + 1) * gd].astype(BF16)
        r = jnp.dot(ug, cs_ref[...], preferred_element_type=F32)
        p_ref[:, g * gd:(g + 1) * gd] = r[:, :gd].astype(BF16)
        q_ref[:, g * gd:(g + 1) * gd] = r[:, gd:].astype(BF16)


def _dft_seq_kernel(c_ref, s_ref, p_ref, q_ref, o_ref):
    acc = jnp.dot(c_ref[...], p_ref[...], preferred_element_type=F32)
    acc = acc - jnp.dot(s_ref[...], q_ref[...], preferred_element_type=F32)
    o_ref[...] = acc.astype(o_ref.dtype)


def _fourier_branch(h, B, S, tm=512):
    T = B * S
    gd = FNET_GROUP_DIM
    cc, sc = _dft_tables(gd, gd ** -0.5)
    cs = jnp.concatenate([cc, sc], axis=1)
    tm1 = min(tm, T)
    p, q = pl.pallas_call(
        _dft_chan_kernel, grid=(T // tm1,),
        in_specs=[pl.BlockSpec((tm1, FNET_WIDTH), lambda i: (i, OFF_U // FNET_WIDTH)),
                  pl.BlockSpec((gd, 2 * gd), lambda i: (0, 0))],
        out_specs=(pl.BlockSpec((tm1, FNET_WIDTH), lambda i: (i, 0)),) * 2,
        out_shape=(jax.ShapeDtypeStruct((T, FNET_WIDTH), BF16),) * 2,
        compiler_params=_cparams(("parallel",), tm1 * FNET_WIDTH * 8), name="dft_chan")(h, cs)
    cseq, sseq = _dft_tables(S, S ** -0.5)
    tm2 = min(tm, S)
    blk = 2 * tm2 * S * 2 + 2 * S * FNET_WIDTH * 2 + tm2 * FNET_WIDTH * 2
    y = pl.pallas_call(
        _dft_seq_kernel, grid=(S // tm2, B),
        in_specs=[pl.BlockSpec((tm2, S), lambda i, b: (i, 0)),
                  pl.BlockSpec((tm2, S), lambda i, b: (i, 0)),
                  pl.BlockSpec((None, S, FNET_WIDTH), lambda i, b: (b, 0, 0)),
                  pl.BlockSpec((None, S, FNET_WIDTH), lambda i, b: (b, 0, 0))],
        out_specs=pl.BlockSpec((None, tm2, FNET_WIDTH), lambda i, b: (b, i, 0)),
        out_shape=jax.ShapeDtypeStruct((B, S, FNET_WIDTH), BF16),
        compiler_params=_cparams(("parallel", "parallel"), blk, tm2 * FNET_WIDTH * 8),
        name="dft_seq")(cseq, sseq, p.reshape(B, S, FNET_WIDTH), q.reshape(B, S, FNET_WIDTH))
    return y.reshape(T, FNET_WIDTH)


def _hgrn_pivots(b, row, reverse):
    C = HGRN_CHUNK
    pivots = []
    for c in (32, 16, 8, 4):
        n = C // (2 * c)
        at = c if reverse else c - 1
        piv = b.reshape(n, 2 * c, LANES)[:, at:at + 1, :]
        pivots.append(jnp.broadcast_to(piv, (n, 2 * c, LANES)).reshape(C, LANES))
    r4 = row & 3
    up1, up2 = pltpu.roll(b, C - 1, 0), pltpu.roll(b, C - 2, 0)
    dn1, dn2 = pltpu.roll(b, 1, 0), pltpu.roll(b, 2, 0)
    if reverse:
        pivots.append(jnp.where(r4 == 0, up2, jnp.where(r4 == 1, up1, jnp.where(r4 == 2, b, dn1))))
        pivots.append(jnp.where((row & 1) == 0, up1, b))
    else:
        pivots.append(jnp.where(r4 == 0, up1, jnp.where(r4 == 1, b, jnp.where(r4 == 2, dn1, dn2))))
        pivots.append(jnp.where((row & 1) == 0, b, dn1))
    return pivots


HGRN_HEADS_PER_STEP = 2


def _hgrn_level_masks(reverse):
    C = HGRN_CHUNK
    tt = lax.broadcasted_iota(jnp.int32, (C, C), 0)
    ss = lax.broadcasted_iota(jnp.int32, (C, C), 1)
    src, dst = (tt, ss) if reverse else (ss, tt)
    masks = [tt == ss]
    for sh in (5, 4, 3, 2, 1, 0):
        sb, db = src >> sh, dst >> sh
        masks.append(((db & 1) == 1) & (sb == db - 1))
    return jnp.stack(masks).astype(F32)


def _hgrn_kernel(q_ref, z_ref, v_ref, lb_ref, mask_ref, o_ref, st_ref, *, n_chunks, reverse):
    C = HGRN_CHUNK

    @pl.when(pl.program_id(2) == 0)
    def _():
        st_ref[...] = jnp.zeros_like(st_ref)

    row = lax.broadcasted_iota(jnp.int32, (C, LANES), 0)
    last = 0 if reverse else C - 1

    def head_chunk(r0, hh):
        ls = slice(hh * LANES, (hh + 1) * LANES)
        lb = lb_ref[:, ls]
        log_lb = jnp.log(jnp.maximum(lb, LB_FLOOR))
        q = q_ref[pl.ds(r0, C), ls]
        z = z_ref[pl.ds(r0, C), ls]
        v = v_ref[pl.ds(r0, C), ls].astype(BF16)
        log_sig = jnp.minimum(z, 0.0) - jnp.log1p(jnp.exp(-jnp.abs(z)))
        t = jnp.log1p(-lb) + log_sig
        logf = jnp.maximum(log_lb, t) + jnp.log1p(jnp.exp(-jnp.abs(log_lb - t)))
        k = (1.0 - lb) / (1.0 + jnp.exp(z))
        b = logf
        for s in (1, 2, 4, 8, 16, 32):
            if reverse:
                b = b + jnp.where(row < C - s, pltpu.roll(b, C - s, 0), 0.0)
            else:
                b = b + jnp.where(row >= s, pltpu.roll(b, s, 0), 0.0)
        qb, kb = q.astype(BF16), k.astype(BF16)
        a = mask_ref[0] * lax.dot_general(qb, kb, NT_DIMS, preferred_element_type=F32)
        for lvl, piv in enumerate(_hgrn_pivots(b, row, reverse)):
            e = jnp.exp(-jnp.abs(b - piv))
            al = lax.dot_general((q * e).astype(BF16), (k * e).astype(BF16), NT_DIMS,
                                 preferred_element_type=F32)
            a = a + mask_ref[lvl + 1] * al
        st = st_ref[hh]
        b_last = b[last:last + 1, :]
        o = jnp.dot(a.astype(BF16), v, preferred_element_type=F32)
        o = o + lax.dot_general((q * jnp.exp(b)).astype(BF16), st.astype(BF16), NT_DIMS,
                                preferred_element_type=F32)
        o_ref[pl.ds(r0, C), ls] = o
        k_tail = (k * jnp.exp(b_last - b)).astype(BF16)
        st_ref[hh] = st * jnp.exp(b_last) + lax.dot_general(v, k_tail, TN_DIMS,
                                                            preferred_element_type=F32)

    def chunk(i, carry):
        ci = n_chunks - 1 - i if reverse else i
        r0 = pl.multiple_of(ci * C, C)
        for hh in range(HGRN_HEADS_PER_STEP):
            head_chunk(r0, hh)
        return carry

    lax.fori_loop(0, n_chunks, chunk, 0, unroll=2)


def _hgrn_scan(src, off_q, off_z, off_v, lb, B, S, reverse, rb=512):
    rb = min(rb, S)
    nblk = S // rb
    nh = HGRN_HEADS_PER_STEP
    w = nh * LANES
    masks = _hgrn_level_masks(reverse)
    blk_idx = (lambda c: nblk - 1 - c) if reverse else (lambda c: c)
    spec = lambda off: pl.BlockSpec((None, rb, w), lambda b, h, c: (b, blk_idx(c), off // w + h))
    return pl.pallas_call(
        functools.partial(_hgrn_kernel, n_chunks=rb // HGRN_CHUNK, reverse=reverse),
        grid=(B, HGRN_HEADS // nh, nblk),
        in_specs=[spec(off_q), spec(off_z), spec(off_v),
                  pl.BlockSpec((None, 1, w), lambda b, h, c: (h, 0, 0)),
                  pl.BlockSpec(masks.shape, lambda b, h, c: (0, 0, 0))],
        out_specs=pl.BlockSpec((None, rb, w), lambda b, h, c: (b, blk_idx(c), h)),
        out_shape=jax.ShapeDtypeStruct((B, S, HGRN_DK), F32),
        scratch_shapes=[pltpu.VMEM((nh, LANES, LANES), F32)],
        compiler_params=_cparams(("parallel", "parallel", "arbitrary"), 4 * rb * w * 4, 2 * MIB),
        name="hgrn_scan_rev" if reverse else "hgrn_scan")(
            src, src, src, lb.reshape(HGRN_HEADS // nh, 1, w).astype(F32), masks)


def _rope(t, cos, sin, perm):
    rot = jnp.dot(t.astype(BF16), perm, preferred_element_type=F32)
    return t * cos + rot * sin


ATTN_GW = ATTN_G * ATTN_DH
ATTN_PAIRS = ATTN_GW // LANES
ATTN_HEADS_PER_PAIR = LANES // ATTN_DH


def _attn_kernel(*refs, L, dil, radius):
    np_ = ATTN_PAIRS
    q_refs, k_refs, v_refs = refs[0:np_], refs[np_:2 * np_], refs[2 * np_:3 * np_]
    cos_ref, sin_ref = refs[3 * np_], refs[3 * np_ + 1]
    o_refs, l_refs = refs[3 * np_ + 2:4 * np_ + 2], refs[4 * np_ + 2:5 * np_ + 2]
    blk = ATTN_BLOCK
    width = blk + 2 * radius
    m0 = pl.program_id(1) * blk
    mstart = jnp.clip(m0 - radius, 0, L - width)
    pr = lax.broadcasted_iota(jnp.int32, (LANES, LANES), 0)
    pc = lax.broadcasted_iota(jnp.int32, (LANES, LANES), 1)
    half = ROPE_DIM // 2
    in_head = pc & (ATTN_DH - 1)
    src = jnp.where(in_head < half, pc + half, jnp.where(in_head < ROPE_DIM, pc - half, -1))
    perm = jnp.where(pr == src, 1.0, 0.0).astype(BF16)
    row = lax.broadcasted_iota(jnp.int32, (blk, width), 0)
    col = lax.broadcasted_iota(jnp.int32, (blk, width), 1)
    valid = jnp.abs(col - row + (mstart - m0)) <= radius
    lane_head = lax.broadcasted_iota(jnp.int32, (blk, LANES), 1) // ATTN_DH

    def rows_of(first, size, c):
        if dil == 1:
            return pl.ds(first if isinstance(first, int) else pl.multiple_of(first, 8), size)
        return pl.ds(first * dil + c, size, stride=dil)

    def residue(c, carry):
        rq = rows_of(0, blk, c)
        rq_abs = rows_of(m0, blk, c)
        rk = rows_of(mstart, width, c)
        cq, sq = cos_ref[rq_abs, :], sin_ref[rq_abs, :]
        ck, sk = cos_ref[rk, :], sin_ref[rk, :]
        for pair in range(np_):
            qr = _rope(q_refs[pair][rq, :], cq, sq, perm)
            kr = _rope(k_refs[pair][rk, :], ck, sk, perm).astype(BF16)
            vb = v_refs[pair][rk, :].astype(BF16)
            out = jnp.zeros((blk, LANES), F32)
            lse = jnp.zeros((blk, LANES), F32)
            for hd in range(ATTN_HEADS_PER_PAIR):
                mine = lane_head == hd
                s = lax.dot_general(jnp.where(mine, qr, 0.0).astype(BF16), kr, NT_DIMS,
                                    preferred_element_type=F32)
                s = jnp.where(valid, s * (1.0 / math.sqrt(ATTN_DH)), NEG_BIG)
                m = jnp.max(s, axis=-1, keepdims=True)
                p = jnp.exp(s - m)
                den = jnp.sum(p, axis=-1, keepdims=True)
                o = jnp.dot(p.astype(BF16), vb, preferred_element_type=F32)
                out = jnp.where(mine, o / den, out)
                lse = jnp.where(mine, m + jnp.log(den), lse)
            o_refs[pair][rq, :] = out
            l_refs[pair][rq, :] = lse
        return carry

    if dil == 1:
        residue(0, 0)
    else:
        lax.fori_loop(0, dil, residue, 0)


def _band_attention(h3, cos, sin, gi, dil, radius):
    B, S, _ = h3.shape
    L = S // dil
    rows = ATTN_BLOCK * dil
    col = lambda off, pair: (off + gi * ATTN_GW) // LANES + pair
    blocked = lambda off, pair: pl.BlockSpec((None, rows, LANES), lambda b, i: (b, i, col(off, pair)))
    full = lambda off, pair: pl.BlockSpec((None, S, LANES), lambda b, i: (b, 0, col(off, pair)))
    tab = pl.BlockSpec((S, LANES), lambda b, i: (0, 0))
    out = pl.BlockSpec((None, rows, LANES), lambda b, i: (b, i, 0))
    pairs = range(ATTN_PAIRS)
    blk = ATTN_PAIRS * (3 * rows + 2 * S) * LANES * 4 + 2 * S * LANES * 4
    shape = jax.ShapeDtypeStruct((B, S, LANES), F32)
    res = pl.pallas_call(
        functools.partial(_attn_kernel, L=L, dil=dil, radius=radius), grid=(B, L // ATTN_BLOCK),
        in_specs=[blocked(OFF_QA, p) for p in pairs] + [full(OFF_KA, p) for p in pairs]
        + [full(OFF_VA, p) for p in pairs] + [tab, tab],
        out_specs=(out,) * (2 * ATTN_PAIRS), out_shape=(shape,) * (2 * ATTN_PAIRS),
        compiler_params=_cparams(("parallel", "parallel"), blk, 4 * MIB),
        name=f"band_attn_d{dil}")(*([h3] * (3 * ATTN_PAIRS)), cos, sin)
    return res[:ATTN_PAIRS], res[ATTN_PAIRS:]


def _attention_branch(h3):
    S = h3.shape[1]
    pos = jnp.arange(S, dtype=F32)
    inv_freq = ROPE_THETA ** (-jnp.arange(0, ROPE_DIM, 2, dtype=F32) / ROPE_DIM)
    ang = pos[:, None] * inv_freq[None, :]
    cos, sin = jnp.cos(ang), jnp.sin(ang)
    rest = ATTN_DH - ROPE_DIM
    reps = (1, ATTN_HEADS_PER_PAIR)
    cos_f = jnp.tile(jnp.concatenate([cos, cos, jnp.ones((S, rest), F32)], axis=1), reps)
    sin_f = jnp.tile(jnp.concatenate([-sin, sin, jnp.zeros((S, rest), F32)], axis=1), reps)
    return [_band_attention(h3, cos_f, sin_f, gi, dil, window // (2 * dil))
            for gi, (window, dil) in enumerate(ATTN_GROUPS)]


def _sigmoid(x):
    return 1.0 / (1.0 + jnp.exp(-x))


def _prep_kernel(of_ref, ob_ref, g0_ref, g1_ref, g2_ref, ng_ref, *rest):
    n_att = len(ATTN_GROUPS) * ATTN_PAIRS
    a_refs, l_refs, (yh_ref, ya_ref) = rest[:n_att], rest[n_att:2 * n_att], rest[2 * n_att:]
    g_refs = (g0_ref, g1_ref, g2_ref)
    for hd in range(HGRN_HEADS):
        sl = slice(hd * LANES, (hd + 1) * LANES)
        o = of_ref[:, sl] + ob_ref[:, sl]
        o = o * lax.rsqrt(jnp.mean(o * o, axis=-1, keepdims=True) + HGRN_NORM_EPS)
        g = g_refs[hd // 2][:, (hd % 2) * LANES:(hd % 2 + 1) * LANES]
        yh_ref[:, sl] = (o * ng_ref[:, sl] * (g * _sigmoid(g))).astype(BF16)
    for pair in range(ATTN_PAIRS):
        outs = [a_refs[g * ATTN_PAIRS + pair][...] for g in range(len(ATTN_GROUPS))]
        lses = [l_refs[g * ATTN_PAIRS + pair][...] for g in range(len(ATTN_GROUPS))]
        mx = functools.reduce(jnp.maximum, lses)
        es = [jnp.exp(l - mx) for l in lses]
        merged = sum(e * o for e, o in zip(es, outs)) / sum(es)
        ya_ref[:, pair * LANES:(pair + 1) * LANES] = merged.astype(BF16)


def _branch_prep(o_f, o_b, h, norm_g, attn, tm=256):
    T = o_f.shape[0]
    rows = lambda w: pl.BlockSpec((tm, w), lambda i: (i, 0))
    blk = tm * (3 * HGRN_DK + 6 * ATTN_GW) * 4 + tm * (HGRN_DK + ATTN_GW) * 2
    gw = 2 * LANES
    gate = lambda j: pl.BlockSpec((tm, gw), lambda i: (i, OFF_G // gw + j))
    outs = [o for os_, _ in attn for o in os_]
    lses = [l for _, ls_ in attn for l in ls_]
    return pl.pallas_call(
        _prep_kernel, grid=(T // tm,),
        in_specs=[rows(HGRN_DK), rows(HGRN_DK), gate(0), gate(1), gate(2),
                  pl.BlockSpec((1, HGRN_DK), lambda i: (0, 0))]
        + [rows(LANES)] * (len(outs) + len(lses)),
        out_specs=(rows(HGRN_DK), rows(ATTN_GW)),
        out_shape=(jax.ShapeDtypeStruct((T, HGRN_DK), BF16),
                   jax.ShapeDtypeStruct((T, ATTN_GW), BF16)),
        compiler_params=_cparams(("parallel",), blk, 2 * MIB), name="branch_prep")(
            o_f, o_b, h, h, h, norm_g.reshape(1, HGRN_DK).astype(F32), *outs, *lses)


def _merge_kernel(yf_ref, yh_ref, ya_ref, wf_ref, wh_ref, wa_ref, g0_ref, g1_ref, g2_ref, o_ref):
    dot = lambda y, w: jnp.dot(y[...], w[...], preferred_element_type=F32)
    acc = _sigmoid(g0_ref[...]) * dot(yf_ref, wf_ref)
    acc = acc + _sigmoid(g1_ref[...]) * dot(yh_ref, wh_ref)
    acc = acc + _sigmoid(g2_ref[...]) * dot(ya_ref, wa_ref)
    o_ref[...] = acc.astype(o_ref.dtype)


def _gated_merge(y_f, y_h, y_a, w_f, w_h, w_a, h, D, tm=512, tn=512):
    T = y_f.shape[0]
    tm = min(tm, T)
    kf, kh, ka = y_f.shape[1], y_h.shape[1], y_a.shape[1]
    rows = lambda w: pl.BlockSpec((tm, w), lambda j, i: (i, 0))
    cols = lambda kk: pl.BlockSpec((kk, tn), lambda j, i: (0, j))
    gate = lambda br: pl.BlockSpec((tm, tn), lambda j, i: (i, (OFF_GATE + br * D) // tn + j))
    blk = tm * (kf + kh + ka) * 2 + (kf + kh + ka) * tn * 2 + 3 * tm * tn * 4 + tm * tn * 2
    return pl.pallas_call(
        _merge_kernel, grid=(D // tn, T // tm),
        in_specs=[rows(kf), rows(kh), rows(ka), cols(kf), cols(kh), cols(ka),
                  gate(0), gate(1), gate(2)],
        out_specs=pl.BlockSpec((tm, tn), lambda j, i: (i, j)),
        out_shape=jax.ShapeDtypeStruct((T, D), BF16),
        compiler_params=_cparams(("parallel", "parallel"), blk, 4 * tm * tn * 4),
        name="gated_merge")(y_f, y_h, y_a, w_f, w_h, w_a, h, h, h)


def _top_rows(s, n, with_rank=False):
    rowi = lax.broadcasted_iota(jnp.int32, (n, s.shape[1]), 0)
    top = jnp.full((n, s.shape[1]), NEG_BIG, F32)
    rank = jnp.full(s.shape, float(n), F32)
    for r in range(n):
        m = jnp.max(s, axis=0, keepdims=True)
        top = jnp.where(rowi == r, m, top)
        hit = s == m
        if with_rank:
            rank = jnp.where(hit, float(r), rank)
        s = jnp.where(hit, NEG_BIG, s)
    return top, jnp.max(s, axis=0, keepdims=True), rank


def _peer_prep_kernel(q_ref, keys_ref, n_ref, a_ref, rk_ref, bp_ref):
    K = PEER_TOPK
    nk = PEER_N_KEYS
    for hd in range(PEER_HEADS):
        sc = []
        for p in range(2):
            qb = q_ref[:, (2 * hd + p) * nk:(2 * hd + p + 1) * nk]
            sc.append(lax.dot_general(keys_ref[p], qb, NT_DIMS, precision=lax.Precision.HIGHEST,
                                      preferred_element_type=F32))
        s1, s2 = sc
        r1, x1, _ = _top_rows(s1, K)
        r2, x2, rank2 = _top_rows(s2, K, with_rank=True)
        cand = jnp.concatenate([r1[0:1, :] + r2] + [r1[a:a + 1, :] + r2[0:8, :] for a in range(1, K)],
                               axis=0)
        ctop, c17, _ = _top_rows(cand, K)
        c16 = ctop[K - 1:K, :]
        c17 = jnp.maximum(c17, jnp.maximum(x1 + r2[0:1, :], r1[0:1, :] + x2))
        tau = 0.5 * (c16 + c17)
        m1, m2 = r1[0:1, :], r2[0:1, :]
        z = jnp.zeros_like(tau)
        for a in range(K):
            ra = r1[a:a + 1, :]
            z = z + jnp.exp(ra - m1) * jnp.sum(
                jnp.where(r2 >= tau - ra, jnp.exp(r2 - m2), 0.0), axis=0, keepdims=True)
        bound = tau - s1
        count = jnp.zeros_like(s1)
        for b in range(K):
            count = count + jnp.where(r2[b:b + 1, :] >= bound, 1.0, 0.0)
        n_ref[hd] = count
        a_ref[hd] = 0.5 * jnp.exp(s1 - m1)
        rk_ref[hd] = rank2.astype(BF16)
        bp_ref[hd] = (jnp.exp(s2 - m2) / z).astype(BF16)


def _peer_prep(qp, sub_keys, tb=256):
    T, W = qp.shape
    tb = min(tb, T)
    nk = PEER_N_KEYS
    out = pl.BlockSpec((PEER_HEADS, nk, tb), lambda i: (0, 0, i))
    blk = tb * W * 4 + 2 * nk * nk * 4 + PEER_HEADS * nk * tb * (4 + 4 + 2 + 2)
    shape = lambda dt: jax.ShapeDtypeStruct((PEER_HEADS, nk, T), dt)
    return pl.pallas_call(
        _peer_prep_kernel, grid=(T // tb,),
        in_specs=[pl.BlockSpec((tb, W), lambda i: (i, 0)),
                  pl.BlockSpec((2, nk, sub_keys.shape[2]), lambda i: (0, 0, 0))],
        out_specs=(out,) * 4,
        out_shape=(shape(F32), shape(F32), shape(BF16), shape(BF16)),
        compiler_params=_cparams(("parallel",), blk, 4 * MIB), name="peer_prep")(qp, sub_keys)


def _peer_main_kernel(x_ref, u_ref, v_ref, n_ref, a_ref, rk_ref, bp_ref, o_ref, p_ref, *,
                      n_part, n_sub):
    nk = PEER_N_KEYS
    e = pl.program_id(1)
    rows = n_sub * nk
    tb = p_ref.shape[1]

    @pl.when(e == 0)
    def _():
        o_ref[...] = jnp.zeros_like(o_ref)

    for part in range(n_part):
        lo = part * rows
        hid = lax.dot_general(u_ref[lo:lo + rows, :], x_ref[...], NT_DIMS,
                              preferred_element_type=F32)
        for j in range(n_sub):
            i1 = (e * n_part + part) * n_sub + j
            ns = [n_ref[hd, pl.ds(i1, 1), :].astype(BF16) for hd in range(PEER_HEADS)]
            avs = [a_ref[hd, pl.ds(i1, 1), :].astype(BF16) for hd in range(PEER_HEADS)]
            for c in range(tb // LANES):
                ls = slice(c * LANES, (c + 1) * LANES)
                w = jnp.zeros((nk, LANES), BF16)
                for hd in range(PEER_HEADS):
                    w = w + jnp.where(rk_ref[hd, :, ls] < ns[hd][:, ls], bp_ref[hd, :, ls],
                                      jnp.zeros((), BF16)) * avs[hd][:, ls]
                hj = hid[j * nk:(j + 1) * nk, ls]
                act = hj * (1.0 + lax.erf(hj * (1.0 / math.sqrt(2.0))))
                p_ref[lo + j * nk:lo + (j + 1) * nk, ls] = act.astype(BF16) * w
        o_ref[...] += lax.dot_general(p_ref[lo:lo + rows, :], v_ref[lo:lo + rows, :], TN_DIMS,
                                      preferred_element_type=F32)


def _peer_main(x_bf, u, v, n, a, rk, bp, tb=512, eb=1024, n_part=2):
    T, D = x_bf.shape
    E = u.shape[0]
    tb = min(tb, T)
    nk = PEER_N_KEYS
    tok = pl.BlockSpec((PEER_HEADS, nk, tb), lambda t, e: (0, 0, t))
    blk = tb * D * 2 + 2 * eb * D * 2 + PEER_HEADS * nk * tb * (4 + 4 + 2 + 2) + tb * D * 4
    return pl.pallas_call(
        functools.partial(_peer_main_kernel, n_part=n_part, n_sub=eb // (n_part * nk)),
        grid=(T // tb, E // eb),
        in_specs=[pl.BlockSpec((tb, D), lambda t, e: (t, 0)),
                  pl.BlockSpec((eb, D), lambda t, e: (e, 0)),
                  pl.BlockSpec((eb, D), lambda t, e: (e, 0)),
                  tok, tok, tok, tok],
        out_specs=pl.BlockSpec((tb, D), lambda t, e: (t, 0)),
        out_shape=jax.ShapeDtypeStruct((T, D), F32),
        scratch_shapes=[pltpu.VMEM((eb, tb), BF16)],
        compiler_params=_cparams(("parallel", "arbitrary"), blk, eb * tb * 2 + 2 * eb * tb * 4),
        name="peer_main")(x_bf, u, v, n, a, rk, bp)


def _peer_ffn(x_bf, w_query, sub_keys, u, v):
    qp = _matmul(x_bf, w_query, F32, 1024, 512, "peer_query")
    n, a, rk, bp = _peer_prep(qp, sub_keys.astype(F32))
    return _peer_main(x_bf, u.astype(BF16), v.astype(BF16), n, a, rk, bp)


def _hybrid_mixer(x_bf, w_in, w_br_f, w_br_h, w_br_a, w_out, lb_f, lb_b, norm_g, B, S):
    T, D = x_bf.shape
    h = _matmul(x_bf, w_in, F32, 1024, 512, "in_proj")
    h3 = h.reshape(B, S, h.shape[1])
    y_f = _fourier_branch(h, B, S)
    o_f = _hgrn_scan(h3, OFF_Q, OFF_FF, OFF_I, lb_f, B, S, reverse=False)
    o_b = _hgrn_scan(h3, OFF_Q, OFF_FB, OFF_I, lb_b, B, S, reverse=True)
    flat = lambda ts: tuple(t.reshape(T, LANES) for t in ts)
    attn = [(flat(os_), flat(ls_)) for os_, ls_ in _attention_branch(h3)]
    y_h, y_a = _branch_prep(o_f.reshape(T, HGRN_DK), o_b.reshape(T, HGRN_DK), h, norm_g, attn)
    merged = _gated_merge(y_f, y_h, y_a, w_br_f.astype(BF16), w_br_h.astype(BF16),
                          w_br_a.astype(BF16), h, D)
    return _matmul(merged, w_out, F32, 1024, 512, "out_proj")


def kernel(x, emb_ln_g, emb_ln_b, w_in, w_br_fourier, w_br_hgrn, w_br_attn, w_out,
           hgrn_lb_logits, hgrn_norm_g, ln_mix_g, ln_mix_b, peer_w_query, peer_sub_keys,
           peer_u, peer_v, ln_ffn_g, ln_ffn_b):
    B, S, D = x.shape
    T = B * S
    depth = w_in.shape[0]
    alpha = (2.0 * depth) ** 0.25
    lb_p = jax.nn.softmax(hgrn_lb_logits.astype(F32), axis=1)
    lb = jnp.cumsum(lb_p, axis=1) - lb_p[:, :1]
    xf, xb = _layer_norm(x.reshape(T, D), emb_ln_g, emb_ln_b)
    for l in range(depth):
        mix = _hybrid_mixer(xb, w_in[l], w_br_fourier[l], w_br_hgrn[l], w_br_attn[l], w_out[l],
                            lb[0, l], lb[1, l], hgrn_norm_g[l], B, S)
        xf, xb = _layer_norm(xf, ln_mix_g[l], ln_mix_b[l], res=mix, alpha=alpha)
        ffn = _peer_ffn(xb, peer_w_query[l], peer_sub_keys[l], peer_u[l], peer_v[l])
        xf, xb = _layer_norm(xf, ln_ffn_g[l], ln_ffn_b[l], res=ffn, alpha=alpha)
    return xf.reshape(B, S, D)
```

```python
import functools
import math

import jax
import jax.numpy as jnp
from jax import lax
from jax.experimental import pallas as pl
from jax.experimental.pallas import tpu as pltpu

F32 = jnp.float32
BF16 = jnp.bfloat16

V7X_VMEM_BYTES = 64 * 1024 * 1024
LANES = 128
BF16_ROWS = 16
MIB = 1024 * 1024

FNET_GROUPS = 4
FNET_GROUP_DIM = 128
FNET_WIDTH = FNET_GROUPS * FNET_GROUP_DIM
HGRN_HEADS = 6
HGRN_DK = HGRN_HEADS * 128
HGRN_CHUNK = 64
HGRN_NORM_EPS = 1e-6
LB_FLOOR = 1e-30
ATTN_GROUPS = ((128, 1), (512, 4), (2048, 16))
ATTN_G = 4
ATTN_DH = 64
ATTN_HEADS = ATTN_G * len(ATTN_GROUPS)
ATTN_WIDTH = ATTN_HEADS * ATTN_DH
ATTN_BLOCK = 128
ROPE_THETA = 500000.0
ROPE_DIM = ATTN_DH // 4
NEG_BIG = -1e30
N_BRANCHES = 3
PEER_HEADS = 8
PEER_N_KEYS = 128
PEER_TOPK = 16
LN_EPS = 1e-5

OFF_U = 0
OFF_FF = OFF_U + FNET_WIDTH
OFF_FB = OFF_FF + HGRN_DK
OFF_Q = OFF_FB + HGRN_DK
OFF_I = OFF_Q + HGRN_DK
OFF_G = OFF_I + HGRN_DK
OFF_QA = OFF_G + HGRN_DK
OFF_KA = OFF_QA + ATTN_WIDTH
OFF_VA = OFF_KA + ATTN_WIDTH
OFF_GATE = OFF_VA + ATTN_WIDTH

NT_DIMS = (((1,), (1,)), ((), ()))
TN_DIMS = (((0,), (0,)), ((), ()))


def _cparams(semantics, block_bytes, temp_bytes=0):
    need = 2 * block_bytes + temp_bytes + 4 * MIB
    limit = int(min(V7X_VMEM_BYTES - 8 * MIB, max(need, 16 * MIB)))
    return pltpu.CompilerParams(dimension_semantics=semantics, vmem_limit_bytes=limit)


def _ln_rows(v, g, b):
    mu = jnp.mean(v, axis=-1, keepdims=True)
    d = v - mu
    var = jnp.mean(d * d, axis=-1, keepdims=True)
    return d * lax.rsqrt(var + LN_EPS) * g + b


def _ln_kernel(x_ref, g_ref, b_ref, of_ref, ob_ref):
    y = _ln_rows(x_ref[...], g_ref[...], b_ref[...])
    of_ref[...] = y
    ob_ref[...] = y.astype(BF16)


def _ln_res_kernel(x_ref, y_ref, g_ref, b_ref, of_ref, ob_ref, *, alpha):
    y = _ln_rows(alpha * x_ref[...] + y_ref[...], g_ref[...], b_ref[...])
    of_ref[...] = y
    ob_ref[...] = y.astype(BF16)


def _layer_norm(x, g, b, res=None, alpha=1.0, tm=256):
    T, D = x.shape
    row = pl.BlockSpec((tm, D), lambda i: (i, 0))
    vec = pl.BlockSpec((1, D), lambda i: (0, 0))
    out_shape = (jax.ShapeDtypeStruct((T, D), F32), jax.ShapeDtypeStruct((T, D), BF16))
    g2, b2 = g.reshape(1, D).astype(F32), b.reshape(1, D).astype(F32)
    blk = tm * D * 4
    if res is None:
        return pl.pallas_call(
            _ln_kernel, grid=(T // tm,), in_specs=[row, vec, vec], out_specs=(row, row),
            out_shape=out_shape, compiler_params=_cparams(("parallel",), 3 * blk, 2 * blk),
            name="ln")(x, g2, b2)
    return pl.pallas_call(
        functools.partial(_ln_res_kernel, alpha=alpha), grid=(T // tm,),
        in_specs=[row, row, vec, vec], out_specs=(row, row), out_shape=out_shape,
        compiler_params=_cparams(("parallel",), 4 * blk, 2 * blk), name="ln_res")(x, res, g2, b2)


def _mm_kernel(a_ref, b_ref, o_ref, wb_ref):
    @pl.when(pl.program_id(1) == 0)
    def _():
        wb_ref[...] = b_ref[...].astype(BF16)

    o_ref[...] = jnp.dot(a_ref[...], wb_ref[...], preferred_element_type=F32).astype(o_ref.dtype)


def _matmul(a, b, layer, out_dtype, tm, tn, name):
    M, K = a.shape
    N = b.shape[2]
    tm, tn = min(tm, M), min(tn, N)
    blk = tm * K * 2 + K * tn * 4 + tm * tn * jnp.dtype(out_dtype).itemsize
    return pl.pallas_call(
        _mm_kernel, grid=(N // tn, M // tm),
        in_specs=[pl.BlockSpec((tm, K), lambda j, i: (i, 0)),
                  pl.BlockSpec((None, K, tn), lambda j, i: (layer, 0, j))],
        out_specs=pl.BlockSpec((tm, tn), lambda j, i: (i, j)),
        out_shape=jax.ShapeDtypeStruct((M, N), out_dtype),
        scratch_shapes=[pltpu.VMEM((K, tn), BF16)],
        compiler_params=_cparams(("parallel", "arbitrary"), blk, tm * tn * 4 + K * tn * 2),
        name=name)(a, b)


def _cast_kernel(w_ref, o_ref):
    o_ref[...] = w_ref[...].astype(o_ref.dtype)


def _cast_layer(w, layer, rb=1024):
    _, R, C = w.shape
    rb = min(rb, R)
    return pl.pallas_call(
        _cast_kernel, grid=(R // rb,),
        in_specs=[pl.BlockSpec((None, rb, C), lambda i: (layer, i, 0))],
        out_specs=pl.BlockSpec((rb, C), lambda i: (i, 0)),
        out_shape=jax.ShapeDtypeStruct((R, C), BF16),
        compiler_params=_cparams(("parallel",), rb * C * 6), name="cast_bf16")(w)


DFT_SPLIT = 64


def _dft_angles(rows, n, period):
    k = jnp.arange(n, dtype=jnp.int32)
    r = jnp.arange(rows, dtype=jnp.int32)
    return ((r[:, None] * k[None, :]) % period).astype(F32) * (2.0 * math.pi / period)


def _dft_tables(n, scale):
    if n <= DFT_SPLIT * 8:
        ang = _dft_angles(n, n, n)
        c, s = jnp.cos(ang), jnp.sin(ang)
    else:
        hi = _dft_angles(n // DFT_SPLIT, n, n // DFT_SPLIT)
        lo = _dft_angles(DFT_SPLIT, n, n)
        ch, sh = jnp.cos(hi)[:, None, :], jnp.sin(hi)[:, None, :]
        cl, sl = jnp.cos(lo)[None, :, :], jnp.sin(lo)[None, :, :]
        c = (ch * cl - sh * sl).reshape(n, n)
        s = (sh * cl + ch * sl).reshape(n, n)
    return (c * scale).astype(BF16), (s * scale).astype(BF16)


def _dft_chan_kernel(u_ref, cs_ref, p_ref, q_ref):
    gd = FNET_GROUP_DIM
    for g in range(FNET_GROUPS):
        ug = u_ref[:, g * gd:(g + 1) * gd].astype(BF16)
        r = jnp.dot(ug, cs_ref[...], preferred_element_type=F32)
        p_ref[:, g * gd:(g + 1) * gd] = r[:, :gd].astype(BF16)
        q_ref[:, g * gd:(g + 1) * gd] = r[:, gd:].astype(BF16)


def _dft_seq_kernel(c_ref, s_ref, p_ref, q_ref, o_ref):
    acc = jnp.dot(c_ref[...], p_ref[...], preferred_element_type=F32)
    acc = acc - jnp.dot(s_ref[...], q_ref[...], preferred_element_type=F32)
    o_ref[...] = acc.astype(o_ref.dtype)


def _fourier_branch(h, B, S, tm=512):
    T = B * S
    gd = FNET_GROUP_DIM
    cc, sc = _dft_tables(gd, gd ** -0.5)
    cs = jnp.concatenate([cc, sc], axis=1)
    tm1 = min(tm, T)
    p, q = pl.pallas_call(
        _dft_chan_kernel, grid=(T // tm1,),
        in_specs=[pl.BlockSpec((tm1, FNET_WIDTH), lambda i: (i, OFF_U // FNET_WIDTH)),
                  pl.BlockSpec((gd, 2 * gd), lambda i: (0, 0))],
        out_specs=(pl.BlockSpec((tm1, FNET_WIDTH), lambda i: (i, 0)),) * 2,
        out_shape=(jax.ShapeDtypeStruct((T, FNET_WIDTH), BF16),) * 2,
        compiler_params=_cparams(("parallel",), tm1 * FNET_WIDTH * 8), name="dft_chan")(h, cs)
    cseq, sseq = _dft_tables(S, S ** -0.5)
    tm2 = min(tm, S)
    blk = 2 * tm2 * S * 2 + 2 * S * FNET_WIDTH * 2 + tm2 * FNET_WIDTH * 2
    y = pl.pallas_call(
        _dft_seq_kernel, grid=(S // tm2, B),
        in_specs=[pl.BlockSpec((tm2, S), lambda i, b: (i, 0)),
                  pl.BlockSpec((tm2, S), lambda i, b: (i, 0)),
                  pl.BlockSpec((None, S, FNET_WIDTH), lambda i, b: (b, 0, 0)),
                  pl.BlockSpec((None, S, FNET_WIDTH), lambda i, b: (b, 0, 0))],
        out_specs=pl.BlockSpec((None, tm2, FNET_WIDTH), lambda i, b: (b, i, 0)),
        out_shape=jax.ShapeDtypeStruct((B, S, FNET_WIDTH), BF16),
        compiler_params=_cparams(("parallel", "parallel"), blk, tm2 * FNET_WIDTH * 8),
        name="dft_seq")(cseq, sseq, p.reshape(B, S, FNET_WIDTH), q.reshape(B, S, FNET_WIDTH))
    return y.reshape(T, FNET_WIDTH)


def _hgrn_pivots(b, row, reverse):
    C = HGRN_CHUNK
    pivots = []
    for c in (32, 16, 8, 4):
        n = C // (2 * c)
        at = c if reverse else c - 1
        piv = b.reshape(n, 2 * c, LANES)[:, at:at + 1, :]
        pivots.append(jnp.broadcast_to(piv, (n, 2 * c, LANES)).reshape(C, LANES))
    r4 = row & 3
    up1, up2 = pltpu.roll(b, C - 1, 0), pltpu.roll(b, C - 2, 0)
    dn1, dn2 = pltpu.roll(b, 1, 0), pltpu.roll(b, 2, 0)
    if reverse:
        pivots.append(jnp.where(r4 == 0, up2, jnp.where(r4 == 1, up1, jnp.where(r4 == 2, b, dn1))))
        pivots.append(jnp.where((row & 1) == 0, up1, b))
    else:
        pivots.append(jnp.where(r4 == 0, up1, jnp.where(r4 == 1, b, jnp.where(r4 == 2, dn1, dn2))))
        pivots.append(jnp.where((row & 1) == 0, b, dn1))
    return pivots


HGRN_HEADS_PER_STEP = 2


def _hgrn_level_masks(reverse):
    C = HGRN_CHUNK
    tt = lax.broadcasted_iota(jnp.int32, (C, C), 0)
    ss = lax.broadcasted_iota(jnp.int32, (C, C), 1)
    src, dst = (tt, ss) if reverse else (ss, tt)
    masks = [tt == ss]
    for sh in (5, 4, 3, 2, 1, 0):
        sb, db = src >> sh, dst >> sh
        masks.append(((db & 1) == 1) & (sb == db - 1))
    return jnp.stack(masks).astype(F32)


def _hgrn_kernel(q_ref, z_ref, v_ref, lb_ref, mask_ref, o_ref, st_ref, *, n_chunks, reverse):
    C = HGRN_CHUNK

    @pl.when(pl.program_id(2) == 0)
    def _():
        st_ref[...] = jnp.zeros_like(st_ref)

    row = lax.broadcasted_iota(jnp.int32, (C, LANES), 0)
    last = 0 if reverse else C - 1

    def head_chunk(r0, hh):
        ls = slice(hh * LANES, (hh + 1) * LANES)
        lb = lb_ref[:, ls]
        log_lb = jnp.log(jnp.maximum(lb, LB_FLOOR))
        q = q_ref[pl.ds(r0, C), ls]
        z = z_ref[pl.ds(r0, C), ls]
        v = v_ref[pl.ds(r0, C), ls].astype(BF16)
        log_sig = jnp.minimum(z, 0.0) - jnp.log1p(jnp.exp(-jnp.abs(z)))
        t = jnp.log1p(-lb) + log_sig
        logf = jnp.maximum(log_lb, t) + jnp.log1p(jnp.exp(-jnp.abs(log_lb - t)))
        k = (1.0 - lb) / (1.0 + jnp.exp(z))
        b = logf
        for s in (1, 2, 4, 8, 16, 32):
            if reverse:
                b = b + jnp.where(row < C - s, pltpu.roll(b, C - s, 0), 0.0)
            else:
                b = b + jnp.where(row >= s, pltpu.roll(b, s, 0), 0.0)
        qb, kb = q.astype(BF16), k.astype(BF16)
        a = mask_ref[0] * lax.dot_general(qb, kb, NT_DIMS, preferred_element_type=F32)
        for lvl, piv in enumerate(_hgrn_pivots(b, row, reverse)):
            e = jnp.exp(-jnp.abs(b - piv))
            al = lax.dot_general((q * e).astype(BF16), (k * e).astype(BF16), NT_DIMS,
                                 preferred_element_type=F32)
            a = a + mask_ref[lvl + 1] * al
        st = st_ref[hh]
        b_last = b[last:last + 1, :]
        o = jnp.dot(a.astype(BF16), v, preferred_element_type=F32)
        o = o + lax.dot_general((q * jnp.exp(b)).astype(BF16), st.astype(BF16), NT_DIMS,
                                preferred_element_type=F32)
        o_ref[pl.ds(r0, C), ls] = o
        k_tail = (k * jnp.exp(b_last - b)).astype(BF16)
        st_ref[hh] = st * jnp.exp(b_last) + lax.dot_general(v, k_tail, TN_DIMS,
                                                            preferred_element_type=F32)

    def chunk(i, carry):
        ci = n_chunks - 1 - i if reverse else i
        r0 = pl.multiple_of(ci * C, C)
        for hh in range(HGRN_HEADS_PER_STEP):
            head_chunk(r0, hh)
        return carry

    lax.fori_loop(0, n_chunks, chunk, 0, unroll=2)


def _hgrn_scan(src, off_q, off_z, off_v, lb, B, S, reverse, rb=512):
    rb = min(rb, S)
    nblk = S // rb
    nh = HGRN_HEADS_PER_STEP
    w = nh * LANES
    masks = _hgrn_level_masks(reverse)
    blk_idx = (lambda c: nblk - 1 - c) if reverse else (lambda c: c)
    spec = lambda off: pl.BlockSpec((None, rb, w), lambda b, h, c: (b, blk_idx(c), off // w + h))
    return pl.pallas_call(
        functools.partial(_hgrn_kernel, n_chunks=rb // HGRN_CHUNK, reverse=reverse),
        grid=(B, HGRN_HEADS // nh, nblk),
        in_specs=[spec(off_q), spec(off_z), spec(off_v),
                  pl.BlockSpec((None, 1, w), lambda b, h, c: (h, 0, 0)),
                  pl.BlockSpec(masks.shape, lambda b, h, c: (0, 0, 0))],
        out_specs=pl.BlockSpec((None, rb, w), lambda b, h, c: (b, blk_idx(c), h)),
        out_shape=jax.ShapeDtypeStruct((B, S, HGRN_DK), F32),
        scratch_shapes=[pltpu.VMEM((nh, LANES, LANES), F32)],
        compiler_params=_cparams(("parallel", "parallel", "arbitrary"), 4 * rb * w * 4, 2 * MIB),
        name="hgrn_scan_rev" if reverse else "hgrn_scan")(
            src, src, src, lb.reshape(HGRN_HEADS // nh, 1, w).astype(F32), masks)


def _rope(t, cos, sin, perm):
    rot = jnp.dot(t.astype(BF16), perm, preferred_element_type=F32)
    return t * cos + rot * sin


ATTN_GW = ATTN_G * ATTN_DH
ATTN_PAIRS = ATTN_GW // LANES
ATTN_HEADS_PER_PAIR = LANES // ATTN_DH


def _attn_kernel(*refs, L, dil, radius):
    np_ = ATTN_PAIRS
    q_refs, k_refs, v_refs = refs[0:np_], refs[np_:2 * np_], refs[2 * np_:3 * np_]
    cos_ref, sin_ref = refs[3 * np_], refs[3 * np_ + 1]
    o_refs, l_refs = refs[3 * np_ + 2:4 * np_ + 2], refs[4 * np_ + 2:5 * np_ + 2]
    blk = ATTN_BLOCK
    width = blk + 2 * radius
    m0 = pl.program_id(1) * blk
    mstart = jnp.clip(m0 - radius, 0, L - width)
    pr = lax.broadcasted_iota(jnp.int32, (LANES, LANES), 0)
    pc = lax.broadcasted_iota(jnp.int32, (LANES, LANES), 1)
    half = ROPE_DIM // 2
    in_head = pc & (ATTN_DH - 1)
    src = jnp.where(in_head < half, pc + half, jnp.where(in_head < ROPE_DIM, pc - half, -1))
    perm = jnp.where(pr == src, 1.0, 0.0).astype(BF16)
    row = lax.broadcasted_iota(jnp.int32, (blk, width), 0)
    col = lax.broadcasted_iota(jnp.int32, (blk, width), 1)
    valid = jnp.abs(col - row + (mstart - m0)) <= radius
    lane_head = lax.broadcasted_iota(jnp.int32, (blk, LANES), 1) // ATTN_DH

    def rows_of(first, size, c):
        if dil == 1:
            return pl.ds(first if isinstance(first, int) else pl.multiple_of(first, 8), size)
        return pl.ds(first * dil + c, size, stride=dil)

    def residue(c, carry):
        rq = rows_of(0, blk, c)
        rq_abs = rows_of(m0, blk, c)
        rk = rows_of(mstart, width, c)
        cq, sq = cos_ref[rq_abs, :], sin_ref[rq_abs, :]
        ck, sk = cos_ref[rk, :], sin_ref[rk, :]
        for pair in range(np_):
            qr = _rope(q_refs[pair][rq, :], cq, sq, perm)
            kr = _rope(k_refs[pair][rk, :], ck, sk, perm).astype(BF16)
            vb = v_refs[pair][rk, :].astype(BF16)
            out = jnp.zeros((blk, LANES), F32)
            lse = jnp.zeros((blk, LANES), F32)
            for hd in range(ATTN_HEADS_PER_PAIR):
                mine = lane_head == hd
                s = lax.dot_general(jnp.where(mine, qr, 0.0).astype(BF16), kr, NT_DIMS,
                                    preferred_element_type=F32)
                s = jnp.where(valid, s * (1.0 / math.sqrt(ATTN_DH)), NEG_BIG)
                m = jnp.max(s, axis=-1, keepdims=True)
                p = jnp.exp(s - m)
                den = jnp.sum(p, axis=-1, keepdims=True)
                o = jnp.dot(p.astype(BF16), vb, preferred_element_type=F32)
                out = jnp.where(mine, o / den, out)
                lse = jnp.where(mine, m + jnp.log(den), lse)
            o_refs[pair][rq, :] = out
            l_refs[pair][rq, :] = lse
        return carry

    if dil == 1:
        residue(0, 0)
    else:
        lax.fori_loop(0, dil, residue, 0)


def _band_attention(h3, cos, sin, gi, dil, radius):
    B, S, _ = h3.shape
    L = S // dil
    rows = ATTN_BLOCK * dil
    col = lambda off, pair: (off + gi * ATTN_GW) // LANES + pair
    blocked = lambda off, pair: pl.BlockSpec((None, rows, LANES), lambda b, i: (b, i, col(off, pair)))
    full = lambda off, pair: pl.BlockSpec((None, S, LANES), lambda b, i: (b, 0, col(off, pair)))
    tab = pl.BlockSpec((S, LANES), lambda b, i: (0, 0))
    out = pl.BlockSpec((None, rows, LANES), lambda b, i: (b, i, 0))
    pairs = range(ATTN_PAIRS)
    blk = ATTN_PAIRS * (3 * rows + 2 * S) * LANES * 4 + 2 * S * LANES * 4
    shape = jax.ShapeDtypeStruct((B, S, LANES), F32)
    res = pl.pallas_call(
        functools.partial(_attn_kernel, L=L, dil=dil, radius=radius), grid=(B, L // ATTN_BLOCK),
        in_specs=[blocked(OFF_QA, p) for p in pairs] + [full(OFF_KA, p) for p in pairs]
        + [full(OFF_VA, p) for p in pairs] + [tab, tab],
        out_specs=(out,) * (2 * ATTN_PAIRS), out_shape=(shape,) * (2 * ATTN_PAIRS),
        compiler_params=_cparams(("parallel", "parallel"), blk, 4 * MIB),
        name=f"band_attn_d{dil}")(*([h3] * (3 * ATTN_PAIRS)), cos, sin)
    return res[:ATTN_PAIRS], res[ATTN_PAIRS:]


def _attention_branch(h3):
    S = h3.shape[1]
    pos = jnp.arange(S, dtype=F32)
    inv_freq = ROPE_THETA ** (-jnp.arange(0, ROPE_DIM, 2, dtype=F32) / ROPE_DIM)
    ang = pos[:, None] * inv_freq[None, :]
    cos, sin = jnp.cos(ang), jnp.sin(ang)
    rest = ATTN_DH - ROPE_DIM
    reps = (1, ATTN_HEADS_PER_PAIR)
    cos_f = jnp.tile(jnp.concatenate([cos, cos, jnp.ones((S, rest), F32)], axis=1), reps)
    sin_f = jnp.tile(jnp.concatenate([-sin, sin, jnp.zeros((S, rest), F32)], axis=1), reps)
    return [_band_attention(h3, cos_f, sin_f, gi, dil, window // (2 * dil))
            for gi, (window, dil) in enumerate(ATTN_GROUPS)]


def _sigmoid(x):
    return 1.0 / (1.0 + jnp.exp(-x))


def _prep_kernel(of_ref, ob_ref, g0_ref, g1_ref, g2_ref, ng_ref, *rest):
    n_att = len(ATTN_GROUPS) * ATTN_PAIRS
    a_refs, l_refs, (yh_ref, ya_ref) = rest[:n_att], rest[n_att:2 * n_att], rest[2 * n_att:]
    g_refs = (g0_ref, g1_ref, g2_ref)
    for hd in range(HGRN_HEADS):
        sl = slice(hd * LANES, (hd + 1) * LANES)
        o = of_ref[:, sl] + ob_ref[:, sl]
        o = o * lax.rsqrt(jnp.mean(o * o, axis=-1, keepdims=True) + HGRN_NORM_EPS)
        g = g_refs[hd // 2][:, (hd % 2) * LANES:(hd % 2 + 1) * LANES]
        yh_ref[:, sl] = (o * ng_ref[:, sl] * (g * _sigmoid(g))).astype(BF16)
    for pair in range(ATTN_PAIRS):
        outs = [a_refs[g * ATTN_PAIRS + pair][...] for g in range(len(ATTN_GROUPS))]
        lses = [l_refs[g * ATTN_PAIRS + pair][...] for g in range(len(ATTN_GROUPS))]
        mx = functools.reduce(jnp.maximum, lses)
        es = [jnp.exp(l - mx) for l in lses]
        merged = sum(e * o for e, o in zip(es, outs)) / sum(es)
        ya_ref[:, pair * LANES:(pair + 1) * LANES] = merged.astype(BF16)


def _branch_prep(o_f, o_b, h, norm_g, attn, tm=256):
    T = o_f.shape[0]
    rows = lambda w: pl.BlockSpec((tm, w), lambda i: (i, 0))
    blk = tm * (3 * HGRN_DK + 6 * ATTN_GW) * 4 + tm * (HGRN_DK + ATTN_GW) * 2
    gw = 2 * LANES
    gate = lambda j: pl.BlockSpec((tm, gw), lambda i: (i, OFF_G // gw + j))
    outs = [o for os_, _ in attn for o in os_]
    lses = [l for _, ls_ in attn for l in ls_]
    return pl.pallas_call(
        _prep_kernel, grid=(T // tm,),
        in_specs=[rows(HGRN_DK), rows(HGRN_DK), gate(0), gate(1), gate(2),
                  pl.BlockSpec((1, HGRN_DK), lambda i: (0, 0))]
        + [rows(LANES)] * (len(outs) + len(lses)),
        out_specs=(rows(HGRN_DK), rows(ATTN_GW)),
        out_shape=(jax.ShapeDtypeStruct((T, HGRN_DK), BF16),
                   jax.ShapeDtypeStruct((T, ATTN_GW), BF16)),
        compiler_params=_cparams(("parallel",), blk, 2 * MIB), name="branch_prep")(
            o_f, o_b, h, h, h, norm_g.reshape(1, HGRN_DK).astype(F32), *outs, *lses)


def _merge_kernel(yf_ref, yh_ref, ya_ref, wf_ref, wh_ref, wa_ref, g0_ref, g1_ref, g2_ref, o_ref):
    dot = lambda y, w: jnp.dot(y[...], w[...], preferred_element_type=F32)
    acc = _sigmoid(g0_ref[...]) * dot(yf_ref, wf_ref)
    acc = acc + _sigmoid(g1_ref[...]) * dot(yh_ref, wh_ref)
    acc = acc + _sigmoid(g2_ref[...]) * dot(ya_ref, wa_ref)
    o_ref[...] = acc.astype(o_ref.dtype)


def _gated_merge(y_f, y_h, y_a, w_f, w_h, w_a, h, D, tm=512, tn=512):
    T = y_f.shape[0]
    tm = min(tm, T)
    kf, kh, ka = y_f.shape[1], y_h.shape[1], y_a.shape[1]
    rows = lambda w: pl.BlockSpec((tm, w), lambda j, i: (i, 0))
    cols = lambda kk: pl.BlockSpec((kk, tn), lambda j, i: (0, j))
    gate = lambda br: pl.BlockSpec((tm, tn), lambda j, i: (i, (OFF_GATE + br * D) // tn + j))
    blk = tm * (kf + kh + ka) * 2 + (kf + kh + ka) * tn * 2 + 3 * tm * tn * 4 + tm * tn * 2
    return pl.pallas_call(
        _merge_kernel, grid=(D // tn, T // tm),
        in_specs=[rows(kf), rows(kh), rows(ka), cols(kf), cols(kh), cols(ka),
                  gate(0), gate(1), gate(2)],
        out_specs=pl.BlockSpec((tm, tn), lambda j, i: (i, j)),
        out_shape=jax.ShapeDtypeStruct((T, D), BF16),
        compiler_params=_cparams(("parallel", "parallel"), blk, 4 * tm * tn * 4),
        name="gated_merge")(y_f, y_h, y_a, w_f, w_h, w_a, h, h, h)


def _top_rows(s, n, with_rank=False):
    rowi = lax.broadcasted_iota(jnp.int32, (n, s.shape[1]), 0)
    top = jnp.full((n, s.shape[1]), NEG_BIG, F32)
    rank = jnp.full(s.shape, float(n), F32)
    for r in range(n):
        m = jnp.max(s, axis=0, keepdims=True)
        top = jnp.where(rowi == r, m, top)
        hit = s == m
        if with_rank:
            rank = jnp.where(hit, float(r), rank)
        s = jnp.where(hit, NEG_BIG, s)
    return top, jnp.max(s, axis=0, keepdims=True), rank


def _peer_prep_kernel(q_ref, keys_ref, n_ref, a_ref, rk_ref, bp_ref):
    K = PEER_TOPK
    nk = PEER_N_KEYS
    for hd in range(PEER_HEADS):
        sc = []
        for p in range(2):
            qb = q_ref[:, (2 * hd + p) * nk:(2 * hd + p + 1) * nk]
            sc.append(lax.dot_general(keys_ref[p], qb, NT_DIMS, precision=lax.Precision.HIGHEST,
                                      preferred_element_type=F32))
        s1, s2 = sc
        r1, x1, _ = _top_rows(s1, K)
        r2, x2, rank2 = _top_rows(s2, K, with_rank=True)
        cand = jnp.concatenate([r1[0:1, :] + r2] + [r1[a:a + 1, :] + r2[0:8, :] for a in range(1, K)],
                               axis=0)
        ctop, c17, _ = _top_rows(cand, K)
        c16 = ctop[K - 1:K, :]
        c17 = jnp.maximum(c17, jnp.maximum(x1 + r2[0:1, :], r1[0:1, :] + x2))
        tau = 0.5 * (c16 + c17)
        m1, m2 = r1[0:1, :], r2[0:1, :]
        z = jnp.zeros_like(tau)
        for a in range(K):
            ra = r1[a:a + 1, :]
            z = z + jnp.exp(ra - m1) * jnp.sum(
                jnp.where(r2 >= tau - ra, jnp.exp(r2 - m2), 0.0), axis=0, keepdims=True)
        bound = tau - s1
        count = jnp.zeros_like(s1)
        for b in range(K):
            count = count + jnp.where(r2[b:b + 1, :] >= bound, 1.0, 0.0)
        n_ref[hd] = count
        a_ref[hd] = 0.5 * jnp.exp(s1 - m1)
        rk_ref[hd] = rank2.astype(BF16)
        bp_ref[hd] = (jnp.exp(s2 - m2) / z).astype(BF16)


def _peer_prep(qp, sub_keys, tb=256):
    T, W = qp.shape
    tb = min(tb, T)
    nk = PEER_N_KEYS
    out = pl.BlockSpec((PEER_HEADS, nk, tb), lambda i: (0, 0, i))
    blk = tb * W * 4 + 2 * nk * nk * 4 + PEER_HEADS * nk * tb * (4 + 4 + 2 + 2)
    shape = lambda dt: jax.ShapeDtypeStruct((PEER_HEADS, nk, T), dt)
    return pl.pallas_call(
        _peer_prep_kernel, grid=(T // tb,),
        in_specs=[pl.BlockSpec((tb, W), lambda i: (i, 0)),
                  pl.BlockSpec((2, nk, sub_keys.shape[2]), lambda i: (0, 0, 0))],
        out_specs=(out,) * 4,
        out_shape=(shape(F32), shape(F32), shape(BF16), shape(BF16)),
        compiler_params=_cparams(("parallel",), blk, 4 * MIB), name="peer_prep")(qp, sub_keys)


def _peer_main_kernel(x_ref, u_ref, v_ref, n_ref, a_ref, rk_ref, bp_ref, o_ref, p_ref, *,
                      n_part, n_sub):
    nk = PEER_N_KEYS
    e = pl.program_id(1)
    rows = n_sub * nk
    tb = p_ref.shape[1]

    @pl.when(e == 0)
    def _():
        o_ref[...] = jnp.zeros_like(o_ref)

    for part in range(n_part):
        lo = part * rows
        hid = lax.dot_general(u_ref[lo:lo + rows, :], x_ref[...], NT_DIMS,
                              preferred_element_type=F32)
        for j in range(n_sub):
            i1 = (e * n_part + part) * n_sub + j
            ns = [n_ref[hd, pl.ds(i1, 1), :] for hd in range(PEER_HEADS)]
            avs = [a_ref[hd, pl.ds(i1, 1), :] for hd in range(PEER_HEADS)]
            for c in range(tb // LANES):
                ls = slice(c * LANES, (c + 1) * LANES)
                spread = lambda r: jnp.broadcast_to(r[:, ls], (BF16_ROWS, LANES)).astype(BF16)[None]
                tiles = lambda ref, hd: ref[hd, :, ls].reshape(nk // BF16_ROWS, BF16_ROWS, LANES)
                w = jnp.zeros((nk // BF16_ROWS, BF16_ROWS, LANES), BF16)
                for hd in range(PEER_HEADS):
                    w = w + jnp.where(tiles(rk_ref, hd) < spread(ns[hd]), tiles(bp_ref, hd),
                                      jnp.zeros((), BF16)) * spread(avs[hd])
                hj = hid[j * nk:(j + 1) * nk, ls]
                act = hj * (1.0 + lax.erf(hj * (1.0 / math.sqrt(2.0))))
                p_ref[lo + j * nk:lo + (j + 1) * nk, ls] = act.astype(BF16) * w.reshape(nk, LANES)
        o_ref[...] += lax.dot_general(p_ref[lo:lo + rows, :], v_ref[lo:lo + rows, :], TN_DIMS,
                                      preferred_element_type=F32)


def _peer_main(x_bf, u, v, n, a, rk, bp, tb=512, eb=1024, n_part=2):
    T, D = x_bf.shape
    E = u.shape[0]
    tb = min(tb, T)
    nk = PEER_N_KEYS
    tok = pl.BlockSpec((PEER_HEADS, nk, tb), lambda t, e: (0, 0, t))
    blk = tb * D * 2 + 2 * eb * D * 2 + PEER_HEADS * nk * tb * (4 + 4 + 2 + 2) + tb * D * 4
    return pl.pallas_call(
        functools.partial(_peer_main_kernel, n_part=n_part, n_sub=eb // (n_part * nk)),
        grid=(T // tb, E // eb),
        in_specs=[pl.BlockSpec((tb, D), lambda t, e: (t, 0)),
                  pl.BlockSpec((eb, D), lambda t, e: (e, 0)),
                  pl.BlockSpec((eb, D), lambda t, e: (e, 0)),
                  tok, tok, tok, tok],
        out_specs=pl.BlockSpec((tb, D), lambda t, e: (t, 0)),
        out_shape=jax.ShapeDtypeStruct((T, D), F32),
        scratch_shapes=[pltpu.VMEM((eb, tb), BF16)],
        compiler_params=_cparams(("parallel", "arbitrary"), blk, eb * tb * 2 + 2 * eb * tb * 4),
        name="peer_main")(x_bf, u, v, n, a, rk, bp)


def _peer_ffn(x_bf, w_query, sub_keys, u, v, layer):
    qp = _matmul(x_bf, w_query, layer, F32, 1024, 512, "peer_query")
    n, a, rk, bp = _peer_prep(qp, sub_keys.astype(F32))
    return _peer_main(x_bf, _cast_layer(u, layer), _cast_layer(v, layer), n, a, rk, bp)


def _hybrid_mixer(x_bf, w_in, w_br_f, w_br_h, w_br_a, w_out, lb_f, lb_b, norm_g, B, S, layer):
    T, D = x_bf.shape
    h = _matmul(x_bf, w_in, layer, F32, 2048, 512, "in_proj")
    h3 = h.reshape(B, S, h.shape[1])
    y_f = _fourier_branch(h, B, S)
    o_f = _hgrn_scan(h3, OFF_Q, OFF_FF, OFF_I, lb_f, B, S, reverse=False)
    o_b = _hgrn_scan(h3, OFF_Q, OFF_FB, OFF_I, lb_b, B, S, reverse=True)
    flat = lambda ts: tuple(t.reshape(T, LANES) for t in ts)
    attn = [(flat(os_), flat(ls_)) for os_, ls_ in _attention_branch(h3)]
    y_h, y_a = _branch_prep(o_f.reshape(T, HGRN_DK), o_b.reshape(T, HGRN_DK), h, norm_g, attn)
    merged = _gated_merge(y_f, y_h, y_a, w_br_f.astype(BF16), w_br_h.astype(BF16),
                          w_br_a.astype(BF16), h, D)
    return _matmul(merged, w_out, layer, F32, 1024, 512, "out_proj")


def kernel(x, emb_ln_g, emb_ln_b, w_in, w_br_fourier, w_br_hgrn, w_br_attn, w_out,
           hgrn_lb_logits, hgrn_norm_g, ln_mix_g, ln_mix_b, peer_w_query, peer_sub_keys,
           peer_u, peer_v, ln_ffn_g, ln_ffn_b):
    B, S, D = x.shape
    T = B * S
    depth = w_in.shape[0]
    alpha = (2.0 * depth) ** 0.25
    lb_p = jax.nn.softmax(hgrn_lb_logits.astype(F32), axis=1)
    lb = jnp.cumsum(lb_p, axis=1) - lb_p[:, :1]
    xf, xb = _layer_norm(x.reshape(T, D), emb_ln_g, emb_ln_b)
    for l in range(depth):
        mix = _hybrid_mixer(xb, w_in, w_br_fourier[l], w_br_hgrn[l], w_br_attn[l], w_out,
                            lb[0, l], lb[1, l], hgrn_norm_g[l], B, S, l)
        xf, xb = _layer_norm(xf, ln_mix_g[l], ln_mix_b[l], res=mix, alpha=alpha)
        ffn = _peer_ffn(xb, peer_w_query, peer_sub_keys[l], peer_u, peer_v, l)
        xf, xb = _layer_norm(xf, ln_ffn_g[l], ln_ffn_b[l], res=ffn, alpha=alpha)
    return xf.reshape(B, S, D)
```

```python
import functools
import math

import jax
import jax.numpy as jnp
from jax import lax
from jax.experimental import pallas as pl
from jax.experimental.pallas import tpu as pltpu

F32 = jnp.float32
BF16 = jnp.bfloat16

V7X_VMEM_BYTES = 64 * 1024 * 1024
LANES = 128
BF16_ROWS = 16
MIB = 1024 * 1024

FNET_GROUPS = 4
FNET_GROUP_DIM = 128
FNET_WIDTH = FNET_GROUPS * FNET_GROUP_DIM
HGRN_HEADS = 6
HGRN_DK = HGRN_HEADS * 128
HGRN_CHUNK = 64
HGRN_NORM_EPS = 1e-6
LB_FLOOR = 1e-30
ATTN_GROUPS = ((128, 1), (512, 4), (2048, 16))
ATTN_G = 4
ATTN_DH = 64
ATTN_HEADS = ATTN_G * len(ATTN_GROUPS)
ATTN_WIDTH = ATTN_HEADS * ATTN_DH
ATTN_BLOCK = 128
ROPE_THETA = 500000.0
ROPE_DIM = ATTN_DH // 4
NEG_BIG = -1e30
N_BRANCHES = 3
PEER_HEADS = 8
PEER_N_KEYS = 128
PEER_TOPK = 16
LN_EPS = 1e-5

OFF_U = 0
OFF_FF = OFF_U + FNET_WIDTH
OFF_FB = OFF_FF + HGRN_DK
OFF_Q = OFF_FB + HGRN_DK
OFF_I = OFF_Q + HGRN_DK
OFF_G = OFF_I + HGRN_DK
OFF_QA = OFF_G + HGRN_DK
OFF_KA = OFF_QA + ATTN_WIDTH
OFF_VA = OFF_KA + ATTN_WIDTH
OFF_GATE = OFF_VA + ATTN_WIDTH

NT_DIMS = (((1,), (1,)), ((), ()))
TN_DIMS = (((0,), (0,)), ((), ()))


def _cparams(semantics, block_bytes, temp_bytes=0):
    need = 2 * block_bytes + temp_bytes + 4 * MIB
    limit = int(min(V7X_VMEM_BYTES - 8 * MIB, max(need, 16 * MIB)))
    return pltpu.CompilerParams(dimension_semantics=semantics, vmem_limit_bytes=limit)


def _ln_rows(v, g, b):
    mu = jnp.mean(v, axis=-1, keepdims=True)
    d = v - mu
    var = jnp.mean(d * d, axis=-1, keepdims=True)
    return d * lax.rsqrt(var + LN_EPS) * g + b


def _ln_kernel(x_ref, g_ref, b_ref, of_ref, ob_ref):
    y = _ln_rows(x_ref[...], g_ref[...], b_ref[...])
    of_ref[...] = y
    ob_ref[...] = y.astype(BF16)


def _ln_res_kernel(x_ref, y_ref, g_ref, b_ref, of_ref, ob_ref, *, alpha):
    y = _ln_rows(alpha * x_ref[...] + y_ref[...], g_ref[...], b_ref[...])
    of_ref[...] = y
    ob_ref[...] = y.astype(BF16)


def _layer_norm(x, g, b, res=None, alpha=1.0, tm=256):
    T, D = x.shape
    row = pl.BlockSpec((tm, D), lambda i: (i, 0))
    vec = pl.BlockSpec((1, D), lambda i: (0, 0))
    out_shape = (jax.ShapeDtypeStruct((T, D), F32), jax.ShapeDtypeStruct((T, D), BF16))
    g2, b2 = g.reshape(1, D).astype(F32), b.reshape(1, D).astype(F32)
    blk = tm * D * 4
    if res is None:
        return pl.pallas_call(
            _ln_kernel, grid=(T // tm,), in_specs=[row, vec, vec], out_specs=(row, row),
            out_shape=out_shape, compiler_params=_cparams(("parallel",), 3 * blk, 2 * blk),
            name="ln")(x, g2, b2)
    return pl.pallas_call(
        functools.partial(_ln_res_kernel, alpha=alpha), grid=(T // tm,),
        in_specs=[row, row, vec, vec], out_specs=(row, row), out_shape=out_shape,
        compiler_params=_cparams(("parallel",), 4 * blk, 2 * blk), name="ln_res")(x, res, g2, b2)


def _mm_kernel(a_ref, b_ref, o_ref, wb_ref):
    @pl.when(pl.program_id(1) == 0)
    def _():
        wb_ref[...] = b_ref[...].astype(BF16)

    o_ref[...] = jnp.dot(a_ref[...], wb_ref[...], preferred_element_type=F32).astype(o_ref.dtype)


def _matmul(a, b, layer, out_dtype, tm, tn, name):
    M, K = a.shape
    N = b.shape[2]
    tm, tn = min(tm, M), min(tn, N)
    blk = tm * K * 2 + K * tn * 4 + tm * tn * jnp.dtype(out_dtype).itemsize
    return pl.pallas_call(
        _mm_kernel, grid=(N // tn, M // tm),
        in_specs=[pl.BlockSpec((tm, K), lambda j, i: (i, 0)),
                  pl.BlockSpec((None, K, tn), lambda j, i: (layer, 0, j))],
        out_specs=pl.BlockSpec((tm, tn), lambda j, i: (i, j)),
        out_shape=jax.ShapeDtypeStruct((M, N), out_dtype),
        scratch_shapes=[pltpu.VMEM((K, tn), BF16)],
        compiler_params=_cparams(("parallel", "arbitrary"), blk, tm * tn * 4 + K * tn * 2),
        name=name)(a, b)


def _cast_kernel(w_ref, o_ref):
    o_ref[...] = w_ref[...].astype(o_ref.dtype)


def _cast_layer(w, layer, rb=1024):
    _, R, C = w.shape
    rb = min(rb, R)
    return pl.pallas_call(
        _cast_kernel, grid=(R // rb,),
        in_specs=[pl.BlockSpec((None, rb, C), lambda i: (layer, i, 0))],
        out_specs=pl.BlockSpec((rb, C), lambda i: (i, 0)),
        out_shape=jax.ShapeDtypeStruct((R, C), BF16),
        compiler_params=_cparams(("parallel",), rb * C * 6), name="cast_bf16")(w)


DFT_SPLIT = 64


def _dft_angles(rows, n, period):
    k = jnp.arange(n, dtype=jnp.int32)
    r = jnp.arange(rows, dtype=jnp.int32)
    return ((r[:, None] * k[None, :]) % period).astype(F32) * (2.0 * math.pi / period)


def _dft_tables(n, scale):
    if n <= DFT_SPLIT * 8:
        ang = _dft_angles(n, n, n)
        c, s = jnp.cos(ang), jnp.sin(ang)
    else:
        hi = _dft_angles(n // DFT_SPLIT, n, n // DFT_SPLIT)
        lo = _dft_angles(DFT_SPLIT, n, n)
        ch, sh = jnp.cos(hi)[:, None, :], jnp.sin(hi)[:, None, :]
        cl, sl = jnp.cos(lo)[None, :, :], jnp.sin(lo)[None, :, :]
        c = (ch * cl - sh * sl).reshape(n, n)
        s = (sh * cl + ch * sl).reshape(n, n)
    return (c * scale).astype(BF16), (s * scale).astype(BF16)


def _dft_chan_kernel(u_ref, cs_ref, p_ref, q_ref):
    gd = FNET_GROUP_DIM
    for g in range(FNET_GROUPS):
        ug = u_ref[:, g * gd:(g + 1) * gd].astype(BF16)
        r = jnp.dot(ug, cs_ref[...], preferred_element_type=F32)
        p_ref[:, g * gd:(g + 1) * gd] = r[:, :gd].astype(BF16)
        q_ref[:, g * gd:(g + 1) * gd] = r[:, gd:].astype(BF16)


def _dft_seq_kernel(c_ref, s_ref, p_ref, q_ref, o_ref):
    acc = jnp.dot(c_ref[...], p_ref[...], preferred_element_type=F32)
    acc = acc - jnp.dot(s_ref[...], q_ref[...], preferred_element_type=F32)
    o_ref[...] = acc.astype(o_ref.dtype)


def _fourier_branch(h, B, S, tm=512):
    T = B * S
    gd = FNET_GROUP_DIM
    cc, sc = _dft_tables(gd, gd ** -0.5)
    cs = jnp.concatenate([cc, sc], axis=1)
    tm1 = min(tm, T)
    p, q = pl.pallas_call(
        _dft_chan_kernel, grid=(T // tm1,),
        in_specs=[pl.BlockSpec((tm1, FNET_WIDTH), lambda i: (i, OFF_U // FNET_WIDTH)),
                  pl.BlockSpec((gd, 2 * gd), lambda i: (0, 0))],
        out_specs=(pl.BlockSpec((tm1, FNET_WIDTH), lambda i: (i, 0)),) * 2,
        out_shape=(jax.ShapeDtypeStruct((T, FNET_WIDTH), BF16),) * 2,
        compiler_params=_cparams(("parallel",), tm1 * FNET_WIDTH * 8), name="dft_chan")(h, cs)
    cseq, sseq = _dft_tables(S, S ** -0.5)
    tm2 = min(tm, S)
    blk = 2 * tm2 * S * 2 + 2 * S * FNET_WIDTH * 2 + tm2 * FNET_WIDTH * 2
    y = pl.pallas_call(
        _dft_seq_kernel, grid=(S // tm2, B),
        in_specs=[pl.BlockSpec((tm2, S), lambda i, b: (i, 0)),
                  pl.BlockSpec((tm2, S), lambda i, b: (i, 0)),
                  pl.BlockSpec((None, S, FNET_WIDTH), lambda i, b: (b, 0, 0)),
                  pl.BlockSpec((None, S, FNET_WIDTH), lambda i, b: (b, 0, 0))],
        out_specs=pl.BlockSpec((None, tm2, FNET_WIDTH), lambda i, b: (b, i, 0)),
        out_shape=jax.ShapeDtypeStruct((B, S, FNET_WIDTH), BF16),
        compiler_params=_cparams(("parallel", "parallel"), blk, tm2 * FNET_WIDTH * 8),
        name="dft_seq")(cseq, sseq, p.reshape(B, S, FNET_WIDTH), q.reshape(B, S, FNET_WIDTH))
    return y.reshape(T, FNET_WIDTH)


def _hgrn_pivots(b, row, reverse):
    C = HGRN_CHUNK
    pivots = []
    for c in (32, 16, 8, 4):
        n = C // (2 * c)
        at = c if reverse else c - 1
        piv = b.reshape(n, 2 * c, LANES)[:, at:at + 1, :]
        pivots.append(jnp.broadcast_to(piv, (n, 2 * c, LANES)).reshape(C, LANES))
    r4 = row & 3
    up1, up2 = pltpu.roll(b, C - 1, 0), pltpu.roll(b, C - 2, 0)
    dn1, dn2 = pltpu.roll(b, 1, 0), pltpu.roll(b, 2, 0)
    if reverse:
        pivots.append(jnp.where(r4 == 0, up2, jnp.where(r4 == 1, up1, jnp.where(r4 == 2, b, dn1))))
        pivots.append(jnp.where((row & 1) == 0, up1, b))
    else:
        pivots.append(jnp.where(r4 == 0, up1, jnp.where(r4 == 1, b, jnp.where(r4 == 2, dn1, dn2))))
        pivots.append(jnp.where((row & 1) == 0, b, dn1))
    return pivots


HGRN_HEADS_PER_STEP = 2


def _hgrn_level_masks(reverse):
    C = HGRN_CHUNK
    tt = lax.broadcasted_iota(jnp.int32, (C, C), 0)
    ss = lax.broadcasted_iota(jnp.int32, (C, C), 1)
    src, dst = (tt, ss) if reverse else (ss, tt)
    masks = [tt == ss]
    for sh in (5, 4, 3, 2, 1, 0):
        sb, db = src >> sh, dst >> sh
        masks.append(((db & 1) == 1) & (sb == db - 1))
    return jnp.stack(masks).astype(F32)


def _hgrn_kernel(q_ref, z_ref, v_ref, lb_ref, mask_ref, o_ref, st_ref, *, n_chunks, reverse):
    C = HGRN_CHUNK

    @pl.when(pl.program_id(2) == 0)
    def _():
        st_ref[...] = jnp.zeros_like(st_ref)

    row = lax.broadcasted_iota(jnp.int32, (C, LANES), 0)
    last = 0 if reverse else C - 1

    def head_chunk(r0, hh):
        ls = slice(hh * LANES, (hh + 1) * LANES)
        lb = lb_ref[:, ls]
        log_lb = jnp.log(jnp.maximum(lb, LB_FLOOR))
        q = q_ref[pl.ds(r0, C), ls]
        z = z_ref[pl.ds(r0, C), ls]
        v = v_ref[pl.ds(r0, C), ls].astype(BF16)
        log_sig = jnp.minimum(z, 0.0) - jnp.log1p(jnp.exp(-jnp.abs(z)))
        t = jnp.log1p(-lb) + log_sig
        logf = jnp.maximum(log_lb, t) + jnp.log1p(jnp.exp(-jnp.abs(log_lb - t)))
        k = (1.0 - lb) / (1.0 + jnp.exp(z))
        b = logf
        for s in (1, 2, 4, 8, 16, 32):
            if reverse:
                b = b + jnp.where(row < C - s, pltpu.roll(b, C - s, 0), 0.0)
            else:
                b = b + jnp.where(row >= s, pltpu.roll(b, s, 0), 0.0)
        qb, kb = q.astype(BF16), k.astype(BF16)
        a = mask_ref[0] * lax.dot_general(qb, kb, NT_DIMS, preferred_element_type=F32)
        for lvl, piv in enumerate(_hgrn_pivots(b, row, reverse)):
            e = jnp.exp(-jnp.abs(b - piv))
            al = lax.dot_general((q * e).astype(BF16), (k * e).astype(BF16), NT_DIMS,
                                 preferred_element_type=F32)
            a = a + mask_ref[lvl + 1] * al
        st = st_ref[hh]
        b_last = b[last:last + 1, :]
        o = jnp.dot(a.astype(BF16), v, preferred_element_type=F32)
        o = o + lax.dot_general((q * jnp.exp(b)).astype(BF16), st.astype(BF16), NT_DIMS,
                                preferred_element_type=F32)
        o_ref[pl.ds(r0, C), ls] = o
        k_tail = (k * jnp.exp(b_last - b)).astype(BF16)
        st_ref[hh] = st * jnp.exp(b_last) + lax.dot_general(v, k_tail, TN_DIMS,
                                                            preferred_element_type=F32)

    def chunk(i, carry):
        ci = n_chunks - 1 - i if reverse else i
        r0 = pl.multiple_of(ci * C, C)
        for hh in range(HGRN_HEADS_PER_STEP):
            head_chunk(r0, hh)
        return carry

    lax.fori_loop(0, n_chunks, chunk, 0, unroll=4)


def _hgrn_scan(src, off_q, off_z, off_v, lb, B, S, reverse, rb=512):
    rb = min(rb, S)
    nblk = S // rb
    nh = HGRN_HEADS_PER_STEP
    w = nh * LANES
    masks = _hgrn_level_masks(reverse)
    blk_idx = (lambda c: nblk - 1 - c) if reverse else (lambda c: c)
    spec = lambda off: pl.BlockSpec((None, rb, w), lambda b, h, c: (b, blk_idx(c), off // w + h))
    return pl.pallas_call(
        functools.partial(_hgrn_kernel, n_chunks=rb // HGRN_CHUNK, reverse=reverse),
        grid=(B, HGRN_HEADS // nh, nblk),
        in_specs=[spec(off_q), spec(off_z), spec(off_v),
                  pl.BlockSpec((None, 1, w), lambda b, h, c: (h, 0, 0)),
                  pl.BlockSpec(masks.shape, lambda b, h, c: (0, 0, 0))],
        out_specs=pl.BlockSpec((None, rb, w), lambda b, h, c: (b, blk_idx(c), h)),
        out_shape=jax.ShapeDtypeStruct((B, S, HGRN_DK), F32),
        scratch_shapes=[pltpu.VMEM((nh, LANES, LANES), F32)],
        compiler_params=_cparams(("parallel", "parallel", "arbitrary"), 4 * rb * w * 4, 2 * MIB),
        name="hgrn_scan_rev" if reverse else "hgrn_scan")(
            src, src, src, lb.reshape(HGRN_HEADS // nh, 1, w).astype(F32), masks)


def _rope(t, cos, sin, perm):
    rot = jnp.dot(t.astype(BF16), perm, preferred_element_type=F32)
    return t * cos + rot * sin


ATTN_GW = ATTN_G * ATTN_DH
ATTN_PAIRS = ATTN_GW // LANES
ATTN_HEADS_PER_PAIR = LANES // ATTN_DH


def _attn_kernel(*refs, L, dil, radius):
    np_ = ATTN_PAIRS
    q_refs, k_refs, v_refs = refs[0:np_], refs[np_:2 * np_], refs[2 * np_:3 * np_]
    cos_ref, sin_ref = refs[3 * np_], refs[3 * np_ + 1]
    o_refs, l_refs = refs[3 * np_ + 2:4 * np_ + 2], refs[4 * np_ + 2:5 * np_ + 2]
    blk = ATTN_BLOCK
    width = blk + 2 * radius
    m0 = pl.program_id(1) * blk
    mstart = jnp.clip(m0 - radius, 0, L - width)
    pr = lax.broadcasted_iota(jnp.int32, (LANES, LANES), 0)
    pc = lax.broadcasted_iota(jnp.int32, (LANES, LANES), 1)
    half = ROPE_DIM // 2
    in_head = pc & (ATTN_DH - 1)
    src = jnp.where(in_head < half, pc + half, jnp.where(in_head < ROPE_DIM, pc - half, -1))
    perm = jnp.where(pr == src, 1.0, 0.0).astype(BF16)
    row = lax.broadcasted_iota(jnp.int32, (blk, width), 0)
    col = lax.broadcasted_iota(jnp.int32, (blk, width), 1)
    valid = jnp.abs(col - row + (mstart - m0)) <= radius
    lane_head = lax.broadcasted_iota(jnp.int32, (blk, LANES), 1) // ATTN_DH

    def rows_of(first, size, c):
        if dil == 1:
            return pl.ds(first if isinstance(first, int) else pl.multiple_of(first, 8), size)
        return pl.ds(first * dil + c, size, stride=dil)

    def residue(c, carry):
        rq = rows_of(0, blk, c)
        rq_abs = rows_of(m0, blk, c)
        rk = rows_of(mstart, width, c)
        cq, sq = cos_ref[rq_abs, :], sin_ref[rq_abs, :]
        ck, sk = cos_ref[rk, :], sin_ref[rk, :]
        for pair in range(np_):
            qr = _rope(q_refs[pair][rq, :], cq, sq, perm)
            kr = _rope(k_refs[pair][rk, :], ck, sk, perm).astype(BF16)
            vb = v_refs[pair][rk, :].astype(BF16)
            out = jnp.zeros((blk, LANES), F32)
            lse = jnp.zeros((blk, LANES), F32)
            for hd in range(ATTN_HEADS_PER_PAIR):
                mine = lane_head == hd
                s = lax.dot_general(jnp.where(mine, qr, 0.0).astype(BF16), kr, NT_DIMS,
                                    preferred_element_type=F32)
                s = jnp.where(valid, s * (1.0 / math.sqrt(ATTN_DH)), NEG_BIG)
                m = jnp.max(s, axis=-1, keepdims=True)
                p = jnp.exp(s - m)
                den = jnp.sum(p, axis=-1, keepdims=True)
                o = jnp.dot(p.astype(BF16), vb, preferred_element_type=F32)
                out = jnp.where(mine, o / den, out)
                lse = jnp.where(mine, m + jnp.log(den), lse)
            o_refs[pair][rq, :] = out
            l_refs[pair][rq, :] = lse
        return carry

    if dil == 1:
        residue(0, 0)
    else:
        lax.fori_loop(0, dil, residue, 0)


def _band_attention(h3, cos, sin, gi, dil, radius):
    B, S, _ = h3.shape
    L = S // dil
    rows = ATTN_BLOCK * dil
    col = lambda off, pair: (off + gi * ATTN_GW) // LANES + pair
    blocked = lambda off, pair: pl.BlockSpec((None, rows, LANES), lambda b, i: (b, i, col(off, pair)))
    full = lambda off, pair: pl.BlockSpec((None, S, LANES), lambda b, i: (b, 0, col(off, pair)))
    tab = pl.BlockSpec((S, LANES), lambda b, i: (0, 0))
    out = pl.BlockSpec((None, rows, LANES), lambda b, i: (b, i, 0))
    pairs = range(ATTN_PAIRS)
    blk = ATTN_PAIRS * (3 * rows + 2 * S) * LANES * 4 + 2 * S * LANES * 4
    shape = jax.ShapeDtypeStruct((B, S, LANES), F32)
    res = pl.pallas_call(
        functools.partial(_attn_kernel, L=L, dil=dil, radius=radius), grid=(B, L // ATTN_BLOCK),
        in_specs=[blocked(OFF_QA, p) for p in pairs] + [full(OFF_KA, p) for p in pairs]
        + [full(OFF_VA, p) for p in pairs] + [tab, tab],
        out_specs=(out,) * (2 * ATTN_PAIRS), out_shape=(shape,) * (2 * ATTN_PAIRS),
        compiler_params=_cparams(("parallel", "parallel"), blk, 4 * MIB),
        name=f"band_attn_d{dil}")(*([h3] * (3 * ATTN_PAIRS)), cos, sin)
    return res[:ATTN_PAIRS], res[ATTN_PAIRS:]


def _attention_branch(h3):
    S = h3.shape[1]
    pos = jnp.arange(S, dtype=F32)
    inv_freq = ROPE_THETA ** (-jnp.arange(0, ROPE_DIM, 2, dtype=F32) / ROPE_DIM)
    ang = pos[:, None] * inv_freq[None, :]
    cos, sin = jnp.cos(ang), jnp.sin(ang)
    rest = ATTN_DH - ROPE_DIM
    reps = (1, ATTN_HEADS_PER_PAIR)
    cos_f = jnp.tile(jnp.concatenate([cos, cos, jnp.ones((S, rest), F32)], axis=1), reps)
    sin_f = jnp.tile(jnp.concatenate([-sin, sin, jnp.zeros((S, rest), F32)], axis=1), reps)
    return [_band_attention(h3, cos_f, sin_f, gi, dil, window // (2 * dil))
            for gi, (window, dil) in enumerate(ATTN_GROUPS)]


def _sigmoid(x):
    return 1.0 / (1.0 + jnp.exp(-x))


def _prep_kernel(of_ref, ob_ref, g0_ref, g1_ref, g2_ref, ng_ref, *rest):
    n_att = len(ATTN_GROUPS) * ATTN_PAIRS
    a_refs, l_refs, (yh_ref, ya_ref) = rest[:n_att], rest[n_att:2 * n_att], rest[2 * n_att:]
    g_refs = (g0_ref, g1_ref, g2_ref)
    for hd in range(HGRN_HEADS):
        sl = slice(hd * LANES, (hd + 1) * LANES)
        o = of_ref[:, sl] + ob_ref[:, sl]
        o = o * lax.rsqrt(jnp.mean(o * o, axis=-1, keepdims=True) + HGRN_NORM_EPS)
        g = g_refs[hd // 2][:, (hd % 2) * LANES:(hd % 2 + 1) * LANES]
        yh_ref[:, sl] = (o * ng_ref[:, sl] * (g * _sigmoid(g))).astype(BF16)
    for pair in range(ATTN_PAIRS):
        outs = [a_refs[g * ATTN_PAIRS + pair][...] for g in range(len(ATTN_GROUPS))]
        lses = [l_refs[g * ATTN_PAIRS + pair][...] for g in range(len(ATTN_GROUPS))]
        mx = functools.reduce(jnp.maximum, lses)
        es = [jnp.exp(l - mx) for l in lses]
        merged = sum(e * o for e, o in zip(es, outs)) / sum(es)
        ya_ref[:, pair * LANES:(pair + 1) * LANES] = merged.astype(BF16)


def _branch_prep(o_f, o_b, h, norm_g, attn, tm=256):
    T = o_f.shape[0]
    rows = lambda w: pl.BlockSpec((tm, w), lambda i: (i, 0))
    blk = tm * (3 * HGRN_DK + 6 * ATTN_GW) * 4 + tm * (HGRN_DK + ATTN_GW) * 2
    gw = 2 * LANES
    gate = lambda j: pl.BlockSpec((tm, gw), lambda i: (i, OFF_G // gw + j))
    outs = [o for os_, _ in attn for o in os_]
    lses = [l for _, ls_ in attn for l in ls_]
    return pl.pallas_call(
        _prep_kernel, grid=(T // tm,),
        in_specs=[rows(HGRN_DK), rows(HGRN_DK), gate(0), gate(1), gate(2),
                  pl.BlockSpec((1, HGRN_DK), lambda i: (0, 0))]
        + [rows(LANES)] * (len(outs) + len(lses)),
        out_specs=(rows(HGRN_DK), rows(ATTN_GW)),
        out_shape=(jax.ShapeDtypeStruct((T, HGRN_DK), BF16),
                   jax.ShapeDtypeStruct((T, ATTN_GW), BF16)),
        compiler_params=_cparams(("parallel",), blk, 2 * MIB), name="branch_prep")(
            o_f, o_b, h, h, h, norm_g.reshape(1, HGRN_DK).astype(F32), *outs, *lses)


def _merge_kernel(yf_ref, yh_ref, ya_ref, wf_ref, wh_ref, wa_ref, g0_ref, g1_ref, g2_ref, o_ref):
    dot = lambda y, w: jnp.dot(y[...], w[...], preferred_element_type=F32)
    acc = _sigmoid(g0_ref[...]) * dot(yf_ref, wf_ref)
    acc = acc + _sigmoid(g1_ref[...]) * dot(yh_ref, wh_ref)
    acc = acc + _sigmoid(g2_ref[...]) * dot(ya_ref, wa_ref)
    o_ref[...] = acc.astype(o_ref.dtype)


def _gated_merge(y_f, y_h, y_a, w_f, w_h, w_a, h, D, tm=1024, tn=512):
    T = y_f.shape[0]
    tm = min(tm, T)
    kf, kh, ka = y_f.shape[1], y_h.shape[1], y_a.shape[1]
    rows = lambda w: pl.BlockSpec((tm, w), lambda j, i: (i, 0))
    cols = lambda kk: pl.BlockSpec((kk, tn), lambda j, i: (0, j))
    gate = lambda br: pl.BlockSpec((tm, tn), lambda j, i: (i, (OFF_GATE + br * D) // tn + j))
    blk = tm * (kf + kh + ka) * 2 + (kf + kh + ka) * tn * 2 + 3 * tm * tn * 4 + tm * tn * 2
    return pl.pallas_call(
        _merge_kernel, grid=(D // tn, T // tm),
        in_specs=[rows(kf), rows(kh), rows(ka), cols(kf), cols(kh), cols(ka),
                  gate(0), gate(1), gate(2)],
        out_specs=pl.BlockSpec((tm, tn), lambda j, i: (i, j)),
        out_shape=jax.ShapeDtypeStruct((T, D), BF16),
        compiler_params=_cparams(("parallel", "parallel"), blk, 4 * tm * tn * 4),
        name="gated_merge")(y_f, y_h, y_a, w_f, w_h, w_a, h, h, h)


def _top_rows(s, n, with_rank=False):
    rowi = lax.broadcasted_iota(jnp.int32, (n, s.shape[1]), 0)
    top = jnp.full((n, s.shape[1]), NEG_BIG, F32)
    rank = jnp.full(s.shape, float(n), F32)
    for r in range(n):
        m = jnp.max(s, axis=0, keepdims=True)
        top = jnp.where(rowi == r, m, top)
        hit = s == m
        if with_rank:
            rank = jnp.where(hit, float(r), rank)
        s = jnp.where(hit, NEG_BIG, s)
    return top, jnp.max(s, axis=0, keepdims=True), rank


def _peer_prep_kernel(q_ref, keys_ref, n_ref, a_ref, rk_ref, bp_ref):
    K = PEER_TOPK
    nk = PEER_N_KEYS
    for hd in range(PEER_HEADS):
        sc = []
        for p in range(2):
            qb = q_ref[:, (2 * hd + p) * nk:(2 * hd + p + 1) * nk]
            sc.append(lax.dot_general(keys_ref[p], qb, NT_DIMS, precision=lax.Precision.HIGHEST,
                                      preferred_element_type=F32))
        s1, s2 = sc
        r1, x1, _ = _top_rows(s1, K)
        r2, x2, rank2 = _top_rows(s2, K, with_rank=True)
        cand = jnp.concatenate([r1[0:1, :] + r2] + [r1[a:a + 1, :] + r2[0:8, :] for a in range(1, K)],
                               axis=0)
        ctop, c17, _ = _top_rows(cand, K)
        c16 = ctop[K - 1:K, :]
        c17 = jnp.maximum(c17, jnp.maximum(x1 + r2[0:1, :], r1[0:1, :] + x2))
        tau = 0.5 * (c16 + c17)
        m1, m2 = r1[0:1, :], r2[0:1, :]
        z = jnp.zeros_like(tau)
        for a in range(K):
            ra = r1[a:a + 1, :]
            z = z + jnp.exp(ra - m1) * jnp.sum(
                jnp.where(r2 >= tau - ra, jnp.exp(r2 - m2), 0.0), axis=0, keepdims=True)
        bound = tau - s1
        count = jnp.zeros_like(s1)
        for b in range(K):
            count = count + jnp.where(r2[b:b + 1, :] >= bound, 1.0, 0.0)
        n_ref[hd] = count
        a_ref[hd] = 0.5 * jnp.exp(s1 - m1)
        rk_ref[hd] = rank2.astype(BF16)
        bp_ref[hd] = (jnp.exp(s2 - m2) / z).astype(BF16)


def _peer_prep(qp, sub_keys, tb=256):
    T, W = qp.shape
    tb = min(tb, T)
    nk = PEER_N_KEYS
    out = pl.BlockSpec((PEER_HEADS, nk, tb), lambda i: (0, 0, i))
    blk = tb * W * 4 + 2 * nk * nk * 4 + PEER_HEADS * nk * tb * (4 + 4 + 2 + 2)
    shape = lambda dt: jax.ShapeDtypeStruct((PEER_HEADS, nk, T), dt)
    return pl.pallas_call(
        _peer_prep_kernel, grid=(T // tb,),
        in_specs=[pl.BlockSpec((tb, W), lambda i: (i, 0)),
                  pl.BlockSpec((2, nk, sub_keys.shape[2]), lambda i: (0, 0, 0))],
        out_specs=(out,) * 4,
        out_shape=(shape(F32), shape(F32), shape(BF16), shape(BF16)),
        compiler_params=_cparams(("parallel",), blk, 4 * MIB), name="peer_prep")(qp, sub_keys)


def _peer_main_kernel(x_ref, u_ref, v_ref, n_ref, a_ref, rk_ref, bp_ref, o_ref, p_ref, *,
                      n_part, n_sub):
    nk = PEER_N_KEYS
    e = pl.program_id(1)
    rows = n_sub * nk
    tb = p_ref.shape[1]

    @pl.when(e == 0)
    def _():
        o_ref[...] = jnp.zeros_like(o_ref)

    for part in range(n_part):
        lo = part * rows
        hid = lax.dot_general(u_ref[lo:lo + rows, :], x_ref[...], NT_DIMS,
                              preferred_element_type=F32)
        for j in range(n_sub):
            i1 = (e * n_part + part) * n_sub + j
            ns = [n_ref[hd, pl.ds(i1, 1), :] for hd in range(PEER_HEADS)]
            avs = [a_ref[hd, pl.ds(i1, 1), :] for hd in range(PEER_HEADS)]
            for c in range(tb // LANES):
                ls = slice(c * LANES, (c + 1) * LANES)
                spread = lambda r: jnp.broadcast_to(r[:, ls], (BF16_ROWS, LANES)).astype(BF16)[None]
                tiles = lambda ref, hd: ref[hd, :, ls].reshape(nk // BF16_ROWS, BF16_ROWS, LANES)
                w = jnp.zeros((nk // BF16_ROWS, BF16_ROWS, LANES), BF16)
                for hd in range(PEER_HEADS):
                    w = w + jnp.where(tiles(rk_ref, hd) < spread(ns[hd]), tiles(bp_ref, hd),
                                      jnp.zeros((), BF16)) * spread(avs[hd])
                hj = hid[j * nk:(j + 1) * nk, ls]
                act = hj * (1.0 + lax.erf(hj * (1.0 / math.sqrt(2.0))))
                p_ref[lo + j * nk:lo + (j + 1) * nk, ls] = act.astype(BF16) * w.reshape(nk, LANES)
        o_ref[...] += lax.dot_general(p_ref[lo:lo + rows, :], v_ref[lo:lo + rows, :], TN_DIMS,
                                      preferred_element_type=F32)


def _peer_main(x_bf, u, v, n, a, rk, bp, tb=512, eb=1024, n_part=1):
    T, D = x_bf.shape
    E = u.shape[0]
    tb = min(tb, T)
    nk = PEER_N_KEYS
    tok = pl.BlockSpec((PEER_HEADS, nk, tb), lambda t, e: (0, 0, t))
    blk = tb * D * 2 + 2 * eb * D * 2 + PEER_HEADS * nk * tb * (4 + 4 + 2 + 2) + tb * D * 4
    return pl.pallas_call(
        functools.partial(_peer_main_kernel, n_part=n_part, n_sub=eb // (n_part * nk)),
        grid=(T // tb, E // eb),
        in_specs=[pl.BlockSpec((tb, D), lambda t, e: (t, 0)),
                  pl.BlockSpec((eb, D), lambda t, e: (e, 0)),
                  pl.BlockSpec((eb, D), lambda t, e: (e, 0)),
                  tok, tok, tok, tok],
        out_specs=pl.BlockSpec((tb, D), lambda t, e: (t, 0)),
        out_shape=jax.ShapeDtypeStruct((T, D), F32),
        scratch_shapes=[pltpu.VMEM((eb, tb), BF16)],
        compiler_params=_cparams(("parallel", "arbitrary"), blk, eb * tb * 2 + 2 * eb * tb * 4),
        name="peer_main")(x_bf, u, v, n, a, rk, bp)


def _peer_ffn(x_bf, w_query, sub_keys, u, v, layer):
    qp = _matmul(x_bf, w_query, layer, F32, 2048, 512, "peer_query")
    n, a, rk, bp = _peer_prep(qp, sub_keys.astype(F32))
    return _peer_main(x_bf, _cast_layer(u, layer), _cast_layer(v, layer), n, a, rk, bp)


def _hybrid_mixer(x_bf, w_in, w_br_f, w_br_h, w_br_a, w_out, lb_f, lb_b, norm_g, B, S, layer):
    T, D = x_bf.shape
    h = _matmul(x_bf, w_in, layer, F32, 2048, 512, "in_proj")
    h3 = h.reshape(B, S, h.shape[1])
    y_f = _fourier_branch(h, B, S)
    o_f = _hgrn_scan(h3, OFF_Q, OFF_FF, OFF_I, lb_f, B, S, reverse=False)
    o_b = _hgrn_scan(h3, OFF_Q, OFF_FB, OFF_I, lb_b, B, S, reverse=True)
    flat = lambda ts: tuple(t.reshape(T, LANES) for t in ts)
    attn = [(flat(os_), flat(ls_)) for os_, ls_ in _attention_branch(h3)]
    y_h, y_a = _branch_prep(o_f.reshape(T, HGRN_DK), o_b.reshape(T, HGRN_DK), h, norm_g, attn)
    merged = _gated_merge(y_f, y_h, y_a, w_br_f.astype(BF16), w_br_h.astype(BF16),
                          w_br_a.astype(BF16), h, D)
    return _matmul(merged, w_out, layer, F32, 2048, 512, "out_proj")


def kernel(x, emb_ln_g, emb_ln_b, w_in, w_br_fourier, w_br_hgrn, w_br_attn, w_out,
           hgrn_lb_logits, hgrn_norm_g, ln_mix_g, ln_mix_b, peer_w_query, peer_sub_keys,
           peer_u, peer_v, ln_ffn_g, ln_ffn_b):
    B, S, D = x.shape
    T = B * S
    depth = w_in.shape[0]
    alpha = (2.0 * depth) ** 0.25
    lb_p = jax.nn.softmax(hgrn_lb_logits.astype(F32), axis=1)
    lb = jnp.cumsum(lb_p, axis=1) - lb_p[:, :1]
    xf, xb = _layer_norm(x.reshape(T, D), emb_ln_g, emb_ln_b)
    for l in range(depth):
        mix = _hybrid_mixer(xb, w_in, w_br_fourier[l], w_br_hgrn[l], w_br_attn[l], w_out,
                            lb[0, l], lb[1, l], hgrn_norm_g[l], B, S, l)
        xf, xb = _layer_norm(xf, ln_mix_g[l], ln_mix_b[l], res=mix, alpha=alpha)
        ffn = _peer_ffn(xb, peer_w_query, peer_sub_keys[l], peer_u, peer_v, l)
        xf, xb = _layer_norm(xf, ln_ffn_g[l], ln_ffn_b[l], res=ffn, alpha=alpha)
    return xf.reshape(B, S, D)
```

```python
import functools
import math

import jax
import jax.numpy as jnp
from jax import lax
from jax.experimental import pallas as pl
from jax.experimental.pallas import tpu as pltpu

F32 = jnp.float32
BF16 = jnp.bfloat16

V7X_VMEM_BYTES = 64 * 1024 * 1024
LANES = 128
BF16_ROWS = 16
MIB = 1024 * 1024

FNET_GROUPS = 4
FNET_GROUP_DIM = 128
FNET_WIDTH = FNET_GROUPS * FNET_GROUP_DIM
HGRN_HEADS = 6
HGRN_DK = HGRN_HEADS * 128
HGRN_CHUNK = 64
HGRN_NORM_EPS = 1e-6
LB_FLOOR = 1e-30
ATTN_GROUPS = ((128, 1), (512, 4), (2048, 16))
ATTN_G = 4
ATTN_DH = 64
ATTN_HEADS = ATTN_G * len(ATTN_GROUPS)
ATTN_WIDTH = ATTN_HEADS * ATTN_DH
ATTN_BLOCK = 128
ROPE_THETA = 500000.0
ROPE_DIM = ATTN_DH // 4
NEG_BIG = -1e30
N_BRANCHES = 3
PEER_HEADS = 8
PEER_N_KEYS = 128
PEER_TOPK = 16
LN_EPS = 1e-5

OFF_U = 0
OFF_FF = OFF_U + FNET_WIDTH
OFF_FB = OFF_FF + HGRN_DK
OFF_Q = OFF_FB + HGRN_DK
OFF_I = OFF_Q + HGRN_DK
OFF_G = OFF_I + HGRN_DK
OFF_QA = OFF_G + HGRN_DK
OFF_KA = OFF_QA + ATTN_WIDTH
OFF_VA = OFF_KA + ATTN_WIDTH
OFF_GATE = OFF_VA + ATTN_WIDTH

NT_DIMS = (((1,), (1,)), ((), ()))
TN_DIMS = (((0,), (0,)), ((), ()))


def _cparams(semantics, block_bytes, temp_bytes=0):
    need = 2 * block_bytes + temp_bytes + 4 * MIB
    limit = int(min(V7X_VMEM_BYTES - 8 * MIB, max(need, 16 * MIB)))
    return pltpu.CompilerParams(dimension_semantics=semantics, vmem_limit_bytes=limit)


def _ln_rows(v, g, b):
    mu = jnp.mean(v, axis=-1, keepdims=True)
    d = v - mu
    var = jnp.mean(d * d, axis=-1, keepdims=True)
    return d * lax.rsqrt(var + LN_EPS) * g + b


def _ln_kernel(x_ref, g_ref, b_ref, of_ref, ob_ref):
    y = _ln_rows(x_ref[...], g_ref[...], b_ref[...])
    of_ref[...] = y
    ob_ref[...] = y.astype(BF16)


def _ln_res_kernel(x_ref, y_ref, g_ref, b_ref, of_ref, ob_ref, *, alpha):
    y = _ln_rows(alpha * x_ref[...] + y_ref[...], g_ref[...], b_ref[...])
    of_ref[...] = y
    ob_ref[...] = y.astype(BF16)


def _layer_norm(x, g, b, res=None, alpha=1.0, tm=256):
    T, D = x.shape
    row = pl.BlockSpec((tm, D), lambda i: (i, 0))
    vec = pl.BlockSpec((1, D), lambda i: (0, 0))
    out_shape = (jax.ShapeDtypeStruct((T, D), F32), jax.ShapeDtypeStruct((T, D), BF16))
    g2, b2 = g.reshape(1, D).astype(F32), b.reshape(1, D).astype(F32)
    blk = tm * D * 4
    if res is None:
        return pl.pallas_call(
            _ln_kernel, grid=(T // tm,), in_specs=[row, vec, vec], out_specs=(row, row),
            out_shape=out_shape, compiler_params=_cparams(("parallel",), 3 * blk, 2 * blk),
            name="ln")(x, g2, b2)
    return pl.pallas_call(
        functools.partial(_ln_res_kernel, alpha=alpha), grid=(T // tm,),
        in_specs=[row, row, vec, vec], out_specs=(row, row), out_shape=out_shape,
        compiler_params=_cparams(("parallel",), 4 * blk, 2 * blk), name="ln_res")(x, res, g2, b2)


def _mm_kernel(a_ref, b_ref, o_ref, wb_ref):
    @pl.when(pl.program_id(1) == 0)
    def _():
        wb_ref[...] = b_ref[...].astype(BF16)

    o_ref[...] = jnp.dot(a_ref[...], wb_ref[...], preferred_element_type=F32).astype(o_ref.dtype)


def _matmul(a, b, layer, out_dtype, tm, tn, name, cols=None):
    M, K = a.shape
    col0, N = cols if cols is not None else (0, b.shape[2])
    tm, tn = min(tm, M), min(tn, N)
    j0 = col0 // tn
    blk = tm * K * 2 + K * tn * 4 + tm * tn * jnp.dtype(out_dtype).itemsize
    return pl.pallas_call(
        _mm_kernel, grid=(N // tn, M // tm),
        in_specs=[pl.BlockSpec((tm, K), lambda j, i: (i, 0)),
                  pl.BlockSpec((None, K, tn), lambda j, i: (layer, 0, j0 + j))],
        out_specs=pl.BlockSpec((tm, tn), lambda j, i: (i, j)),
        out_shape=jax.ShapeDtypeStruct((M, N), out_dtype),
        scratch_shapes=[pltpu.VMEM((K, tn), BF16)],
        compiler_params=_cparams(("parallel", "arbitrary"), blk, tm * tn * 4 + K * tn * 2),
        name=name)(a, b)


def _cast_kernel(w_ref, o_ref):
    o_ref[...] = w_ref[...].astype(o_ref.dtype)


def _cast_layer(w, layer, rb=1024):
    _, R, C = w.shape
    rb = min(rb, R)
    return pl.pallas_call(
        _cast_kernel, grid=(R // rb,),
        in_specs=[pl.BlockSpec((None, rb, C), lambda i: (layer, i, 0))],
        out_specs=pl.BlockSpec((rb, C), lambda i: (i, 0)),
        out_shape=jax.ShapeDtypeStruct((R, C), BF16),
        compiler_params=_cparams(("parallel",), rb * C * 6), name="cast_bf16")(w)


DFT_SPLIT = 64


def _dft_angles(rows, n, period):
    k = jnp.arange(n, dtype=jnp.int32)
    r = jnp.arange(rows, dtype=jnp.int32)
    return ((r[:, None] * k[None, :]) % period).astype(F32) * (2.0 * math.pi / period)


def _dft_tables(n, scale):
    if n <= DFT_SPLIT * 8:
        ang = _dft_angles(n, n, n)
        c, s = jnp.cos(ang), jnp.sin(ang)
    else:
        hi = _dft_angles(n // DFT_SPLIT, n, n // DFT_SPLIT)
        lo = _dft_angles(DFT_SPLIT, n, n)
        ch, sh = jnp.cos(hi)[:, None, :], jnp.sin(hi)[:, None, :]
        cl, sl = jnp.cos(lo)[None, :, :], jnp.sin(lo)[None, :, :]
        c = (ch * cl - sh * sl).reshape(n, n)
        s = (sh * cl + ch * sl).reshape(n, n)
    return (c * scale).astype(BF16), (s * scale).astype(BF16)


def _dft_chan_kernel(u_ref, cs_ref, p_ref, q_ref):
    gd = FNET_GROUP_DIM
    for g in range(FNET_GROUPS):
        ug = u_ref[:, g * gd:(g + 1) * gd].astype(BF16)
        r = jnp.dot(ug, cs_ref[...], preferred_element_type=F32)
        p_ref[:, g * gd:(g + 1) * gd] = r[:, :gd].astype(BF16)
        q_ref[:, g * gd:(g + 1) * gd] = r[:, gd:].astype(BF16)


def _dft_seq_kernel(c_ref, s_ref, p_ref, q_ref, o_ref):
    acc = jnp.dot(c_ref[...], p_ref[...], preferred_element_type=F32)
    acc = acc - jnp.dot(s_ref[...], q_ref[...], preferred_element_type=F32)
    o_ref[...] = acc.astype(o_ref.dtype)


def _fourier_branch(h, B, S, tm=512):
    T = B * S
    gd = FNET_GROUP_DIM
    cc, sc = _dft_tables(gd, gd ** -0.5)
    cs = jnp.concatenate([cc, sc], axis=1)
    tm1 = min(tm, T)
    p, q = pl.pallas_call(
        _dft_chan_kernel, grid=(T // tm1,),
        in_specs=[pl.BlockSpec((tm1, FNET_WIDTH), lambda i: (i, OFF_U // FNET_WIDTH)),
                  pl.BlockSpec((gd, 2 * gd), lambda i: (0, 0))],
        out_specs=(pl.BlockSpec((tm1, FNET_WIDTH), lambda i: (i, 0)),) * 2,
        out_shape=(jax.ShapeDtypeStruct((T, FNET_WIDTH), BF16),) * 2,
        compiler_params=_cparams(("parallel",), tm1 * FNET_WIDTH * 8), name="dft_chan")(h, cs)
    cseq, sseq = _dft_tables(S, S ** -0.5)
    tm2 = min(tm, S)
    blk = 2 * tm2 * S * 2 + 2 * S * FNET_WIDTH * 2 + tm2 * FNET_WIDTH * 2
    y = pl.pallas_call(
        _dft_seq_kernel, grid=(S // tm2, B),
        in_specs=[pl.BlockSpec((tm2, S), lambda i, b: (i, 0)),
                  pl.BlockSpec((tm2, S), lambda i, b: (i, 0)),
                  pl.BlockSpec((None, S, FNET_WIDTH), lambda i, b: (b, 0, 0)),
                  pl.BlockSpec((None, S, FNET_WIDTH), lambda i, b: (b, 0, 0))],
        out_specs=pl.BlockSpec((None, tm2, FNET_WIDTH), lambda i, b: (b, i, 0)),
        out_shape=jax.ShapeDtypeStruct((B, S, FNET_WIDTH), BF16),
        compiler_params=_cparams(("parallel", "parallel"), blk, tm2 * FNET_WIDTH * 8),
        name="dft_seq")(cseq, sseq, p.reshape(B, S, FNET_WIDTH), q.reshape(B, S, FNET_WIDTH))
    return y.reshape(T, FNET_WIDTH)


def _hgrn_pivots(b, row, reverse):
    C = HGRN_CHUNK
    pivots = []
    for c in (32, 16, 8, 4):
        n = C // (2 * c)
        at = c if reverse else c - 1
        piv = b.reshape(n, 2 * c, LANES)[:, at:at + 1, :]
        pivots.append(jnp.broadcast_to(piv, (n, 2 * c, LANES)).reshape(C, LANES))
    r4 = row & 3
    up1, up2 = pltpu.roll(b, C - 1, 0), pltpu.roll(b, C - 2, 0)
    dn1, dn2 = pltpu.roll(b, 1, 0), pltpu.roll(b, 2, 0)
    if reverse:
        pivots.append(jnp.where(r4 == 0, up2, jnp.where(r4 == 1, up1, jnp.where(r4 == 2, b, dn1))))
        pivots.append(jnp.where((row & 1) == 0, up1, b))
    else:
        pivots.append(jnp.where(r4 == 0, up1, jnp.where(r4 == 1, b, jnp.where(r4 == 2, dn1, dn2))))
        pivots.append(jnp.where((row & 1) == 0, b, dn1))
    return pivots


HGRN_HEADS_PER_STEP = 2


def _hgrn_level_masks(reverse):
    C = HGRN_CHUNK
    tt = lax.broadcasted_iota(jnp.int32, (C, C), 0)
    ss = lax.broadcasted_iota(jnp.int32, (C, C), 1)
    src, dst = (tt, ss) if reverse else (ss, tt)
    masks = [tt == ss]
    for sh in (5, 4, 3, 2, 1, 0):
        sb, db = src >> sh, dst >> sh
        masks.append(((db & 1) == 1) & (sb == db - 1))
    return jnp.stack(masks).astype(F32)


def _hgrn_kernel(q_ref, z_ref, v_ref, lb_ref, mask_ref, o_ref, st_ref, *, n_chunks, reverse):
    C = HGRN_CHUNK

    @pl.when(pl.program_id(2) == 0)
    def _():
        st_ref[...] = jnp.zeros_like(st_ref)

    row = lax.broadcasted_iota(jnp.int32, (C, LANES), 0)
    last = 0 if reverse else C - 1

    def head_chunk(r0, hh):
        ls = slice(hh * LANES, (hh + 1) * LANES)
        lb = lb_ref[:, ls]
        log_lb = jnp.log(jnp.maximum(lb, LB_FLOOR))
        q = q_ref[pl.ds(r0, C), ls]
        z = z_ref[pl.ds(r0, C), ls]
        v = v_ref[pl.ds(r0, C), ls].astype(BF16)
        log_sig = jnp.minimum(z, 0.0) - jnp.log1p(jnp.exp(-jnp.abs(z)))
        t = jnp.log1p(-lb) + log_sig
        logf = jnp.maximum(log_lb, t) + jnp.log1p(jnp.exp(-jnp.abs(log_lb - t)))
        k = (1.0 - lb) / (1.0 + jnp.exp(z))
        b = logf
        for s in (1, 2, 4, 8, 16, 32):
            if reverse:
                b = b + jnp.where(row < C - s, pltpu.roll(b, C - s, 0), 0.0)
            else:
                b = b + jnp.where(row >= s, pltpu.roll(b, s, 0), 0.0)
        qb, kb = q.astype(BF16), k.astype(BF16)
        a = mask_ref[0] * lax.dot_general(qb, kb, NT_DIMS, preferred_element_type=F32)
        for lvl, piv in enumerate(_hgrn_pivots(b, row, reverse)):
            e = jnp.exp(-jnp.abs(b - piv))
            al = lax.dot_general((q * e).astype(BF16), (k * e).astype(BF16), NT_DIMS,
                                 preferred_element_type=F32)
            a = a + mask_ref[lvl + 1] * al
        st = st_ref[hh]
        b_last = b[last:last + 1, :]
        o = jnp.dot(a.astype(BF16), v, preferred_element_type=F32)
        o = o + lax.dot_general((q * jnp.exp(b)).astype(BF16), st.astype(BF16), NT_DIMS,
                                preferred_element_type=F32)
        o_ref[pl.ds(r0, C), ls] = o
        k_tail = (k * jnp.exp(b_last - b)).astype(BF16)
        st_ref[hh] = st * jnp.exp(b_last) + lax.dot_general(v, k_tail, TN_DIMS,
                                                            preferred_element_type=F32)

    def chunk(i, carry):
        ci = n_chunks - 1 - i if reverse else i
        r0 = pl.multiple_of(ci * C, C)
        for hh in range(HGRN_HEADS_PER_STEP):
            head_chunk(r0, hh)
        return carry

    lax.fori_loop(0, n_chunks, chunk, 0, unroll=4)


def _hgrn_scan(src, off_q, off_z, off_v, lb, B, S, reverse, rb=512):
    rb = min(rb, S)
    nblk = S // rb
    nh = HGRN_HEADS_PER_STEP
    w = nh * LANES
    masks = _hgrn_level_masks(reverse)
    blk_idx = (lambda c: nblk - 1 - c) if reverse else (lambda c: c)
    spec = lambda off: pl.BlockSpec((None, rb, w), lambda b, h, c: (b, blk_idx(c), off // w + h))
    return pl.pallas_call(
        functools.partial(_hgrn_kernel, n_chunks=rb // HGRN_CHUNK, reverse=reverse),
        grid=(B, HGRN_HEADS // nh, nblk),
        in_specs=[spec(off_q), spec(off_z), spec(off_v),
                  pl.BlockSpec((None, 1, w), lambda b, h, c: (h, 0, 0)),
                  pl.BlockSpec(masks.shape, lambda b, h, c: (0, 0, 0))],
        out_specs=pl.BlockSpec((None, rb, w), lambda b, h, c: (b, blk_idx(c), h)),
        out_shape=jax.ShapeDtypeStruct((B, S, HGRN_DK), F32),
        scratch_shapes=[pltpu.VMEM((nh, LANES, LANES), F32)],
        compiler_params=_cparams(("parallel", "parallel", "arbitrary"), 4 * rb * w * 4, 2 * MIB),
        name="hgrn_scan_rev" if reverse else "hgrn_scan")(
            src, src, src, lb.reshape(HGRN_HEADS // nh, 1, w).astype(F32), masks)


def _rope(t, cos, sin, perm):
    rot = jnp.dot(t.astype(BF16), perm, preferred_element_type=F32)
    return t * cos + rot * sin


ATTN_GW = ATTN_G * ATTN_DH
ATTN_PAIRS = ATTN_GW // LANES
ATTN_HEADS_PER_PAIR = LANES // ATTN_DH


def _attn_kernel(*refs, L, dil, radius):
    np_ = ATTN_PAIRS
    q_refs, k_refs, v_refs = refs[0:np_], refs[np_:2 * np_], refs[2 * np_:3 * np_]
    cos_ref, sin_ref = refs[3 * np_], refs[3 * np_ + 1]
    o_refs, l_refs = refs[3 * np_ + 2:4 * np_ + 2], refs[4 * np_ + 2:5 * np_ + 2]
    blk = ATTN_BLOCK
    width = blk + 2 * radius
    m0 = pl.program_id(1) * blk
    mstart = jnp.clip(m0 - radius, 0, L - width)
    pr = lax.broadcasted_iota(jnp.int32, (LANES, LANES), 0)
    pc = lax.broadcasted_iota(jnp.int32, (LANES, LANES), 1)
    half = ROPE_DIM // 2
    in_head = pc & (ATTN_DH - 1)
    src = jnp.where(in_head < half, pc + half, jnp.where(in_head < ROPE_DIM, pc - half, -1))
    perm = jnp.where(pr == src, 1.0, 0.0).astype(BF16)
    row = lax.broadcasted_iota(jnp.int32, (blk, width), 0)
    col = lax.broadcasted_iota(jnp.int32, (blk, width), 1)
    valid = jnp.abs(col - row + (mstart - m0)) <= radius
    lane_head = lax.broadcasted_iota(jnp.int32, (blk, LANES), 1) // ATTN_DH

    def rows_of(first, size, c):
        if dil == 1:
            return pl.ds(first if isinstance(first, int) else pl.multiple_of(first, 8), size)
        return pl.ds(first * dil + c, size, stride=dil)

    def residue(c, carry):
        rq = rows_of(0, blk, c)
        rq_abs = rows_of(m0, blk, c)
        rk = rows_of(mstart, width, c)
        cq, sq = cos_ref[rq_abs, :], sin_ref[rq_abs, :]
        ck, sk = cos_ref[rk, :], sin_ref[rk, :]
        for pair in range(np_):
            qr = _rope(q_refs[pair][rq, :], cq, sq, perm)
            kr = _rope(k_refs[pair][rk, :], ck, sk, perm).astype(BF16)
            vb = v_refs[pair][rk, :].astype(BF16)
            out = jnp.zeros((blk, LANES), F32)
            lse = jnp.zeros((blk, LANES), F32)
            for hd in range(ATTN_HEADS_PER_PAIR):
                mine = lane_head == hd
                s = lax.dot_general(jnp.where(mine, qr, 0.0).astype(BF16), kr, NT_DIMS,
                                    preferred_element_type=F32)
                s = jnp.where(valid, s * (1.0 / math.sqrt(ATTN_DH)), NEG_BIG)
                m = jnp.max(s, axis=-1, keepdims=True)
                p = jnp.exp(s - m)
                den = jnp.sum(p, axis=-1, keepdims=True)
                o = jnp.dot(p.astype(BF16), vb, preferred_element_type=F32)
                out = jnp.where(mine, o / den, out)
                lse = jnp.where(mine, m + jnp.log(den), lse)
            o_refs[pair][rq, :] = out
            l_refs[pair][rq, :] = lse
        return carry

    if dil == 1:
        residue(0, 0)
    else:
        lax.fori_loop(0, dil, residue, 0)


def _band_attention(h3, cos, sin, gi, dil, radius):
    B, S, _ = h3.shape
    L = S // dil
    rows = ATTN_BLOCK * dil
    col = lambda off, pair: (off + gi * ATTN_GW) // LANES + pair
    blocked = lambda off, pair: pl.BlockSpec((None, rows, LANES), lambda b, i: (b, i, col(off, pair)))
    full = lambda off, pair: pl.BlockSpec((None, S, LANES), lambda b, i: (b, 0, col(off, pair)))
    tab = pl.BlockSpec((S, LANES), lambda b, i: (0, 0))
    out = pl.BlockSpec((None, rows, LANES), lambda b, i: (b, i, 0))
    pairs = range(ATTN_PAIRS)
    blk = ATTN_PAIRS * (3 * rows + 2 * S) * LANES * 4 + 2 * S * LANES * 4
    shape = jax.ShapeDtypeStruct((B, S, LANES), F32)
    res = pl.pallas_call(
        functools.partial(_attn_kernel, L=L, dil=dil, radius=radius), grid=(B, L // ATTN_BLOCK),
        in_specs=[blocked(OFF_QA, p) for p in pairs] + [full(OFF_KA, p) for p in pairs]
        + [full(OFF_VA, p) for p in pairs] + [tab, tab],
        out_specs=(out,) * (2 * ATTN_PAIRS), out_shape=(shape,) * (2 * ATTN_PAIRS),
        compiler_params=_cparams(("parallel", "parallel"), blk, 4 * MIB),
        name=f"band_attn_d{dil}")(*([h3] * (3 * ATTN_PAIRS)), cos, sin)
    return res[:ATTN_PAIRS], res[ATTN_PAIRS:]


def _attention_branch(h3):
    S = h3.shape[1]
    pos = jnp.arange(S, dtype=F32)
    inv_freq = ROPE_THETA ** (-jnp.arange(0, ROPE_DIM, 2, dtype=F32) / ROPE_DIM)
    ang = pos[:, None] * inv_freq[None, :]
    cos, sin = jnp.cos(ang), jnp.sin(ang)
    rest = ATTN_DH - ROPE_DIM
    reps = (1, ATTN_HEADS_PER_PAIR)
    cos_f = jnp.tile(jnp.concatenate([cos, cos, jnp.ones((S, rest), F32)], axis=1), reps)
    sin_f = jnp.tile(jnp.concatenate([-sin, sin, jnp.zeros((S, rest), F32)], axis=1), reps)
    return [_band_attention(h3, cos_f, sin_f, gi, dil, window // (2 * dil))
            for gi, (window, dil) in enumerate(ATTN_GROUPS)]


def _sigmoid(x):
    return 1.0 / (1.0 + jnp.exp(-x))


def _prep_kernel(of_ref, ob_ref, g0_ref, g1_ref, g2_ref, ng_ref, *rest):
    n_att = len(ATTN_GROUPS) * ATTN_PAIRS
    a_refs, l_refs, (yh_ref, ya_ref) = rest[:n_att], rest[n_att:2 * n_att], rest[2 * n_att:]
    g_refs = (g0_ref, g1_ref, g2_ref)
    for hd in range(HGRN_HEADS):
        sl = slice(hd * LANES, (hd + 1) * LANES)
        o = of_ref[:, sl] + ob_ref[:, sl]
        o = o * lax.rsqrt(jnp.mean(o * o, axis=-1, keepdims=True) + HGRN_NORM_EPS)
        g = g_refs[hd // 2][:, (hd % 2) * LANES:(hd % 2 + 1) * LANES]
        yh_ref[:, sl] = (o * ng_ref[:, sl] * (g * _sigmoid(g))).astype(BF16)
    for pair in range(ATTN_PAIRS):
        outs = [a_refs[g * ATTN_PAIRS + pair][...] for g in range(len(ATTN_GROUPS))]
        lses = [l_refs[g * ATTN_PAIRS + pair][...] for g in range(len(ATTN_GROUPS))]
        mx = functools.reduce(jnp.maximum, lses)
        es = [jnp.exp(l - mx) for l in lses]
        merged = sum(e * o for e, o in zip(es, outs)) / sum(es)
        ya_ref[:, pair * LANES:(pair + 1) * LANES] = merged.astype(BF16)


def _branch_prep(o_f, o_b, h, norm_g, attn, tm=256):
    T = o_f.shape[0]
    rows = lambda w: pl.BlockSpec((tm, w), lambda i: (i, 0))
    blk = tm * (3 * HGRN_DK + 6 * ATTN_GW) * 4 + tm * (HGRN_DK + ATTN_GW) * 2
    gw = 2 * LANES
    gate = lambda j: pl.BlockSpec((tm, gw), lambda i: (i, OFF_G // gw + j))
    outs = [o for os_, _ in attn for o in os_]
    lses = [l for _, ls_ in attn for l in ls_]
    return pl.pallas_call(
        _prep_kernel, grid=(T // tm,),
        in_specs=[rows(HGRN_DK), rows(HGRN_DK), gate(0), gate(1), gate(2),
                  pl.BlockSpec((1, HGRN_DK), lambda i: (0, 0))]
        + [rows(LANES)] * (len(outs) + len(lses)),
        out_specs=(rows(HGRN_DK), rows(ATTN_GW)),
        out_shape=(jax.ShapeDtypeStruct((T, HGRN_DK), BF16),
                   jax.ShapeDtypeStruct((T, ATTN_GW), BF16)),
        compiler_params=_cparams(("parallel",), blk, 2 * MIB), name="branch_prep")(
            o_f, o_b, h, h, h, norm_g.reshape(1, HGRN_DK).astype(F32), *outs, *lses)


def _merge_kernel(yf_ref, yh_ref, ya_ref, wf_ref, wh_ref, wa_ref, g0_ref, g1_ref, g2_ref, o_ref):
    dot = lambda y, w: jnp.dot(y[...], w[...], preferred_element_type=F32)
    gate = lambda g_ref: _sigmoid(g_ref[...].astype(F32))
    acc = gate(g0_ref) * dot(yf_ref, wf_ref)
    acc = acc + gate(g1_ref) * dot(yh_ref, wh_ref)
    acc = acc + gate(g2_ref) * dot(ya_ref, wa_ref)
    o_ref[...] = acc.astype(o_ref.dtype)


def _gated_merge(y_f, y_h, y_a, w_f, w_h, w_a, gates, D, tm=1024, tn=512):
    T = y_f.shape[0]
    tm = min(tm, T)
    kf, kh, ka = y_f.shape[1], y_h.shape[1], y_a.shape[1]
    rows = lambda w: pl.BlockSpec((tm, w), lambda j, i: (i, 0))
    cols = lambda kk: pl.BlockSpec((kk, tn), lambda j, i: (0, j))
    gate = lambda br: pl.BlockSpec((tm, tn), lambda j, i: (i, br * D // tn + j))
    blk = tm * (kf + kh + ka) * 2 + (kf + kh + ka) * tn * 2 + 3 * tm * tn * 2 + tm * tn * 2
    return pl.pallas_call(
        _merge_kernel, grid=(D // tn, T // tm),
        in_specs=[rows(kf), rows(kh), rows(ka), cols(kf), cols(kh), cols(ka),
                  gate(0), gate(1), gate(2)],
        out_specs=pl.BlockSpec((tm, tn), lambda j, i: (i, j)),
        out_shape=jax.ShapeDtypeStruct((T, D), BF16),
        compiler_params=_cparams(("parallel", "parallel"), blk, 4 * tm * tn * 4),
        name="gated_merge")(y_f, y_h, y_a, w_f, w_h, w_a, gates, gates, gates)


def _top_rows(s, n, with_rank=False):
    rowi = lax.broadcasted_iota(jnp.int32, (n, s.shape[1]), 0)
    top = jnp.full((n, s.shape[1]), NEG_BIG, F32)
    rank = jnp.full(s.shape, float(n), F32)
    for r in range(n):
        m = jnp.max(s, axis=0, keepdims=True)
        top = jnp.where(rowi == r, m, top)
        hit = s == m
        if with_rank:
            rank = jnp.where(hit, float(r), rank)
        s = jnp.where(hit, NEG_BIG, s)
    return top, jnp.max(s, axis=0, keepdims=True), rank


def _peer_prep_kernel(q_ref, keys_ref, n_ref, a_ref, rk_ref, bp_ref):
    K = PEER_TOPK
    nk = PEER_N_KEYS
    for hd in range(PEER_HEADS):
        sc = []
        for p in range(2):
            qb = q_ref[:, (2 * hd + p) * nk:(2 * hd + p + 1) * nk]
            sc.append(lax.dot_general(keys_ref[p], qb, NT_DIMS, precision=lax.Precision.HIGHEST,
                                      preferred_element_type=F32))
        s1, s2 = sc
        r1, x1, _ = _top_rows(s1, K)
        r2, x2, rank2 = _top_rows(s2, K, with_rank=True)
        cand = jnp.concatenate([r1[0:1, :] + r2] + [r1[a:a + 1, :] + r2[0:8, :] for a in range(1, K)],
                               axis=0)
        ctop, c17, _ = _top_rows(cand, K)
        c16 = ctop[K - 1:K, :]
        c17 = jnp.maximum(c17, jnp.maximum(x1 + r2[0:1, :], r1[0:1, :] + x2))
        tau = 0.5 * (c16 + c17)
        m1, m2 = r1[0:1, :], r2[0:1, :]
        z = jnp.zeros_like(tau)
        for a in range(K):
            ra = r1[a:a + 1, :]
            z = z + jnp.exp(ra - m1) * jnp.sum(
                jnp.where(r2 >= tau - ra, jnp.exp(r2 - m2), 0.0), axis=0, keepdims=True)
        bound = tau - s1
        count = jnp.zeros_like(s1)
        for b in range(K):
            count = count + jnp.where(r2[b:b + 1, :] >= bound, 1.0, 0.0)
        n_ref[hd] = count
        a_ref[hd] = 0.5 * jnp.exp(s1 - m1)
        rk_ref[hd] = rank2.astype(BF16)
        bp_ref[hd] = (jnp.exp(s2 - m2) / z).astype(BF16)


def _peer_prep(qp, sub_keys, tb=256):
    T, W = qp.shape
    tb = min(tb, T)
    nk = PEER_N_KEYS
    out = pl.BlockSpec((PEER_HEADS, nk, tb), lambda i: (0, 0, i))
    blk = tb * W * 4 + 2 * nk * nk * 4 + PEER_HEADS * nk * tb * (4 + 4 + 2 + 2)
    shape = lambda dt: jax.ShapeDtypeStruct((PEER_HEADS, nk, T), dt)
    return pl.pallas_call(
        _peer_prep_kernel, grid=(T // tb,),
        in_specs=[pl.BlockSpec((tb, W), lambda i: (i, 0)),
                  pl.BlockSpec((2, nk, sub_keys.shape[2]), lambda i: (0, 0, 0))],
        out_specs=(out,) * 4,
        out_shape=(shape(F32), shape(F32), shape(BF16), shape(BF16)),
        compiler_params=_cparams(("parallel",), blk, 4 * MIB), name="peer_prep")(qp, sub_keys)


def _peer_main_kernel(x_ref, u_ref, v_ref, n_ref, a_ref, rk_ref, bp_ref, o_ref, p_ref, *,
                      n_part, n_sub):
    nk = PEER_N_KEYS
    e = pl.program_id(1)
    rows = n_sub * nk
    tb = p_ref.shape[1]

    @pl.when(e == 0)
    def _():
        o_ref[...] = jnp.zeros_like(o_ref)

    for part in range(n_part):
        lo = part * rows
        hid = lax.dot_general(u_ref[lo:lo + rows, :], x_ref[...], NT_DIMS,
                              preferred_element_type=F32)
        for j in range(n_sub):
            i1 = (e * n_part + part) * n_sub + j
            ns = [n_ref[hd, pl.ds(i1, 1), :] for hd in range(PEER_HEADS)]
            avs = [a_ref[hd, pl.ds(i1, 1), :] for hd in range(PEER_HEADS)]
            for c in range(tb // LANES):
                ls = slice(c * LANES, (c + 1) * LANES)
                spread = lambda r: jnp.broadcast_to(r[:, ls], (BF16_ROWS, LANES)).astype(BF16)[None]
                tiles = lambda ref, hd: ref[hd, :, ls].reshape(nk // BF16_ROWS, BF16_ROWS, LANES)
                w = jnp.zeros((nk // BF16_ROWS, BF16_ROWS, LANES), BF16)
                for hd in range(PEER_HEADS):
                    w = w + jnp.where(tiles(rk_ref, hd) < spread(ns[hd]), tiles(bp_ref, hd),
                                      jnp.zeros((), BF16)) * spread(avs[hd])
                hj = hid[j * nk:(j + 1) * nk, ls]
                act = hj * (1.0 + lax.erf(hj * (1.0 / math.sqrt(2.0))))
                p_ref[lo + j * nk:lo + (j + 1) * nk, ls] = act.astype(BF16) * w.reshape(nk, LANES)
        o_ref[...] += lax.dot_general(p_ref[lo:lo + rows, :], v_ref[lo:lo + rows, :], TN_DIMS,
                                      preferred_element_type=F32)


def _peer_main(x_bf, u, v, n, a, rk, bp, tb=512, eb=1024, n_part=1):
    T, D = x_bf.shape
    E = u.shape[0]
    tb = min(tb, T)
    nk = PEER_N_KEYS
    tok = pl.BlockSpec((PEER_HEADS, nk, tb), lambda t, e: (0, 0, t))
    blk = tb * D * 2 + 2 * eb * D * 2 + PEER_HEADS * nk * tb * (4 + 4 + 2 + 2) + tb * D * 4
    return pl.pallas_call(
        functools.partial(_peer_main_kernel, n_part=n_part, n_sub=eb // (n_part * nk)),
        grid=(T // tb, E // eb),
        in_specs=[pl.BlockSpec((tb, D), lambda t, e: (t, 0)),
                  pl.BlockSpec((eb, D), lambda t, e: (e, 0)),
                  pl.BlockSpec((eb, D), lambda t, e: (e, 0)),
                  tok, tok, tok, tok],
        out_specs=pl.BlockSpec((tb, D), lambda t, e: (t, 0)),
        out_shape=jax.ShapeDtypeStruct((T, D), F32),
        scratch_shapes=[pltpu.VMEM((eb, tb), BF16)],
        compiler_params=_cparams(("parallel", "arbitrary"), blk, eb * tb * 2 + 2 * eb * tb * 4),
        name="peer_main")(x_bf, u, v, n, a, rk, bp)


def _peer_ffn(x_bf, w_query, sub_keys, u, v, layer):
    qp = _matmul(x_bf, w_query, layer, F32, 2048, 512, "peer_query")
    n, a, rk, bp = _peer_prep(qp, sub_keys.astype(F32))
    return _peer_main(x_bf, _cast_layer(u, layer), _cast_layer(v, layer), n, a, rk, bp)


def _hybrid_mixer(x_bf, w_in, w_br_f, w_br_h, w_br_a, w_out, lb_f, lb_b, norm_g, B, S, layer):
    T, D = x_bf.shape
    h = _matmul(x_bf, w_in, layer, F32, 2048, 512, "in_proj", cols=(0, OFF_GATE))
    gates = _matmul(x_bf, w_in, layer, BF16, 2048, 512, "in_proj_gates",
                    cols=(OFF_GATE, N_BRANCHES * D))
    h3 = h.reshape(B, S, h.shape[1])
    y_f = _fourier_branch(h, B, S)
    o_f = _hgrn_scan(h3, OFF_Q, OFF_FF, OFF_I, lb_f, B, S, reverse=False)
    o_b = _hgrn_scan(h3, OFF_Q, OFF_FB, OFF_I, lb_b, B, S, reverse=True)
    flat = lambda ts: tuple(t.reshape(T, LANES) for t in ts)
    attn = [(flat(os_), flat(ls_)) for os_, ls_ in _attention_branch(h3)]
    y_h, y_a = _branch_prep(o_f.reshape(T, HGRN_DK), o_b.reshape(T, HGRN_DK), h, norm_g, attn)
    merged = _gated_merge(y_f, y_h, y_a, w_br_f.astype(BF16), w_br_h.astype(BF16),
                          w_br_a.astype(BF16), gates, D)
    return _matmul(merged, w_out, layer, F32, 2048, 512, "out_proj")


def kernel(x, emb_ln_g, emb_ln_b, w_in, w_br_fourier, w_br_hgrn, w_br_attn, w_out,
           hgrn_lb_logits, hgrn_norm_g, ln_mix_g, ln_mix_b, peer_w_query, peer_sub_keys,
           peer_u, peer_v, ln_ffn_g, ln_ffn_b):
    B, S, D = x.shape
    T = B * S
    depth = w_in.shape[0]
    alpha = (2.0 * depth) ** 0.25
    lb_p = jax.nn.softmax(hgrn_lb_logits.astype(F32), axis=1)
    lb = jnp.cumsum(lb_p, axis=1) - lb_p[:, :1]
    xf, xb = _layer_norm(x.reshape(T, D), emb_ln_g, emb_ln_b)
    for l in range(depth):
        mix = _hybrid_mixer(xb, w_in, w_br_fourier[l], w_br_hgrn[l], w_br_attn[l], w_out,
                            lb[0, l], lb[1, l], hgrn_norm_g[l], B, S, l)
        xf, xb = _layer_norm(xf, ln_mix_g[l], ln_mix_b[l], res=mix, alpha=alpha)
        ffn = _peer_ffn(xb, peer_w_query, peer_sub_keys[l], peer_u, peer_v, l)
        xf, xb = _layer_norm(xf, ln_ffn_g[l], ln_ffn_b[l], res=ffn, alpha=alpha)
    return xf.reshape(B, S, D)
```

```python
import functools
import math

import jax
import jax.numpy as jnp
from jax import lax
from jax.experimental import pallas as pl
from jax.experimental.pallas import tpu as pltpu

F32 = jnp.float32
BF16 = jnp.bfloat16

V7X_VMEM_BYTES = 64 * 1024 * 1024
LANES = 128
BF16_ROWS = 16
MIB = 1024 * 1024

FNET_GROUPS = 4
FNET_GROUP_DIM = 128
FNET_WIDTH = FNET_GROUPS * FNET_GROUP_DIM
HGRN_HEADS = 6
HGRN_DK = HGRN_HEADS * 128
HGRN_CHUNK = 64
HGRN_NORM_EPS = 1e-6
LB_FLOOR = 1e-30
ATTN_GROUPS = ((128, 1), (512, 4), (2048, 16))
ATTN_G = 4
ATTN_DH = 64
ATTN_HEADS = ATTN_G * len(ATTN_GROUPS)
ATTN_WIDTH = ATTN_HEADS * ATTN_DH
ATTN_BLOCK = 128
ROPE_THETA = 500000.0
ROPE_DIM = ATTN_DH // 4
NEG_BIG = -1e30
N_BRANCHES = 3
PEER_HEADS = 8
PEER_N_KEYS = 128
PEER_TOPK = 16
LN_EPS = 1e-5

OFF_U = 0
OFF_FF = OFF_U + FNET_WIDTH
OFF_FB = OFF_FF + HGRN_DK
OFF_Q = OFF_FB + HGRN_DK
OFF_I = OFF_Q + HGRN_DK
OFF_G = OFF_I + HGRN_DK
OFF_QA = OFF_G + HGRN_DK
OFF_KA = OFF_QA + ATTN_WIDTH
OFF_VA = OFF_KA + ATTN_WIDTH
OFF_GATE = OFF_VA + ATTN_WIDTH

NT_DIMS = (((1,), (1,)), ((), ()))
TN_DIMS = (((0,), (0,)), ((), ()))


def _cparams(semantics, block_bytes, temp_bytes=0):
    need = 2 * block_bytes + temp_bytes + 4 * MIB
    limit = int(min(V7X_VMEM_BYTES - 8 * MIB, max(need, 16 * MIB)))
    return pltpu.CompilerParams(dimension_semantics=semantics, vmem_limit_bytes=limit)


def _ln_rows(v, g, b):
    mu = jnp.mean(v, axis=-1, keepdims=True)
    d = v - mu
    var = jnp.mean(d * d, axis=-1, keepdims=True)
    return d * lax.rsqrt(var + LN_EPS) * g + b


def _ln_kernel(x_ref, g_ref, b_ref, of_ref, ob_ref):
    y = _ln_rows(x_ref[...], g_ref[...], b_ref[...])
    of_ref[...] = y
    ob_ref[...] = y.astype(BF16)


def _ln_res_kernel(x_ref, y_ref, g_ref, b_ref, of_ref, ob_ref, *, alpha):
    y = _ln_rows(alpha * x_ref[...] + y_ref[...], g_ref[...], b_ref[...])
    of_ref[...] = y
    ob_ref[...] = y.astype(BF16)


def _layer_norm(x, g, b, res=None, alpha=1.0, tm=256):
    T, D = x.shape
    row = pl.BlockSpec((tm, D), lambda i: (i, 0))
    vec = pl.BlockSpec((1, D), lambda i: (0, 0))
    out_shape = (jax.ShapeDtypeStruct((T, D), F32), jax.ShapeDtypeStruct((T, D), BF16))
    g2, b2 = g.reshape(1, D).astype(F32), b.reshape(1, D).astype(F32)
    blk = tm * D * 4
    if res is None:
        return pl.pallas_call(
            _ln_kernel, grid=(T // tm,), in_specs=[row, vec, vec], out_specs=(row, row),
            out_shape=out_shape, compiler_params=_cparams(("parallel",), 3 * blk, 2 * blk),
            name="ln")(x, g2, b2)
    return pl.pallas_call(
        functools.partial(_ln_res_kernel, alpha=alpha), grid=(T // tm,),
        in_specs=[row, row, vec, vec], out_specs=(row, row), out_shape=out_shape,
        compiler_params=_cparams(("parallel",), 4 * blk, 2 * blk), name="ln_res")(x, res, g2, b2)


def _mm_kernel(a_ref, b_ref, o_ref, wb_ref):
    @pl.when(pl.program_id(1) == 0)
    def _():
        wb_ref[...] = b_ref[...].astype(BF16)

    o_ref[...] = jnp.dot(a_ref[...], wb_ref[...], preferred_element_type=F32).astype(o_ref.dtype)


def _matmul(a, b, layer, out_dtype, tm, tn, name, cols=None):
    M, K = a.shape
    col0, N = cols if cols is not None else (0, b.shape[2])
    tm, tn = min(tm, M), min(tn, N)
    j0 = col0 // tn
    blk = tm * K * 2 + K * tn * 4 + tm * tn * jnp.dtype(out_dtype).itemsize
    return pl.pallas_call(
        _mm_kernel, grid=(N // tn, M // tm),
        in_specs=[pl.BlockSpec((tm, K), lambda j, i: (i, 0)),
                  pl.BlockSpec((None, K, tn), lambda j, i: (layer, 0, j0 + j))],
        out_specs=pl.BlockSpec((tm, tn), lambda j, i: (i, j)),
        out_shape=jax.ShapeDtypeStruct((M, N), out_dtype),
        scratch_shapes=[pltpu.VMEM((K, tn), BF16)],
        compiler_params=_cparams(("parallel", "arbitrary"), blk, tm * tn * 4 + K * tn * 2),
        name=name)(a, b)


def _cast_kernel(w_ref, o_ref):
    o_ref[...] = w_ref[...].astype(o_ref.dtype)


def _cast_layer(w, layer, rb=1024):
    _, R, C = w.shape
    rb = min(rb, R)
    return pl.pallas_call(
        _cast_kernel, grid=(R // rb,),
        in_specs=[pl.BlockSpec((None, rb, C), lambda i: (layer, i, 0))],
        out_specs=pl.BlockSpec((rb, C), lambda i: (i, 0)),
        out_shape=jax.ShapeDtypeStruct((R, C), BF16),
        compiler_params=_cparams(("parallel",), rb * C * 6), name="cast_bf16")(w)


DFT_SPLIT = 64


def _dft_angles(rows, n, period):
    k = jnp.arange(n, dtype=jnp.int32)
    r = jnp.arange(rows, dtype=jnp.int32)
    return ((r[:, None] * k[None, :]) % period).astype(F32) * (2.0 * math.pi / period)


def _dft_tables(n, scale):
    if n <= DFT_SPLIT * 8:
        ang = _dft_angles(n, n, n)
        c, s = jnp.cos(ang), jnp.sin(ang)
    else:
        hi = _dft_angles(n // DFT_SPLIT, n, n // DFT_SPLIT)
        lo = _dft_angles(DFT_SPLIT, n, n)
        ch, sh = jnp.cos(hi)[:, None, :], jnp.sin(hi)[:, None, :]
        cl, sl = jnp.cos(lo)[None, :, :], jnp.sin(lo)[None, :, :]
        c = (ch * cl - sh * sl).reshape(n, n)
        s = (sh * cl + ch * sl).reshape(n, n)
    return (c * scale).astype(BF16), (s * scale).astype(BF16)


def _dft_chan_kernel(u_ref, cs_ref, p_ref, q_ref):
    gd = FNET_GROUP_DIM
    for g in range(FNET_GROUPS):
        ug = u_ref[:, g * gd:(g + 1) * gd].astype(BF16)
        r = jnp.dot(ug, cs_ref[...], preferred_element_type=F32)
        p_ref[:, g * gd:(g + 1) * gd] = r[:, :gd].astype(BF16)
        q_ref[:, g * gd:(g + 1) * gd] = r[:, gd:].astype(BF16)


def _dft_seq_kernel(c_ref, s_ref, p_ref, q_ref, o_ref):
    acc = jnp.dot(c_ref[...], p_ref[...], preferred_element_type=F32)
    acc = acc - jnp.dot(s_ref[...], q_ref[...], preferred_element_type=F32)
    o_ref[...] = acc.astype(o_ref.dtype)


def _fourier_branch(h, B, S, tm=512):
    T = B * S
    gd = FNET_GROUP_DIM
    cc, sc = _dft_tables(gd, gd ** -0.5)
    cs = jnp.concatenate([cc, sc], axis=1)
    tm1 = min(tm, T)
    p, q = pl.pallas_call(
        _dft_chan_kernel, grid=(T // tm1,),
        in_specs=[pl.BlockSpec((tm1, FNET_WIDTH), lambda i: (i, OFF_U // FNET_WIDTH)),
                  pl.BlockSpec((gd, 2 * gd), lambda i: (0, 0))],
        out_specs=(pl.BlockSpec((tm1, FNET_WIDTH), lambda i: (i, 0)),) * 2,
        out_shape=(jax.ShapeDtypeStruct((T, FNET_WIDTH), BF16),) * 2,
        compiler_params=_cparams(("parallel",), tm1 * FNET_WIDTH * 8), name="dft_chan")(h, cs)
    cseq, sseq = _dft_tables(S, S ** -0.5)
    tm2 = min(tm, S)
    blk = 2 * tm2 * S * 2 + 2 * S * FNET_WIDTH * 2 + tm2 * FNET_WIDTH * 2
    y = pl.pallas_call(
        _dft_seq_kernel, grid=(S // tm2, B),
        in_specs=[pl.BlockSpec((tm2, S), lambda i, b: (i, 0)),
                  pl.BlockSpec((tm2, S), lambda i, b: (i, 0)),
                  pl.BlockSpec((None, S, FNET_WIDTH), lambda i, b: (b, 0, 0)),
                  pl.BlockSpec((None, S, FNET_WIDTH), lambda i, b: (b, 0, 0))],
        out_specs=pl.BlockSpec((None, tm2, FNET_WIDTH), lambda i, b: (b, i, 0)),
        out_shape=jax.ShapeDtypeStruct((B, S, FNET_WIDTH), BF16),
        compiler_params=_cparams(("parallel", "parallel"), blk, tm2 * FNET_WIDTH * 8),
        name="dft_seq")(cseq, sseq, p.reshape(B, S, FNET_WIDTH), q.reshape(B, S, FNET_WIDTH))
    return y.reshape(T, FNET_WIDTH)


def _hgrn_pivots(b, row, reverse):
    C = HGRN_CHUNK
    pivots = []
    for c in (32, 16, 8, 4):
        n = C // (2 * c)
        at = c if reverse else c - 1
        piv = b.reshape(n, 2 * c, LANES)[:, at:at + 1, :]
        pivots.append(jnp.broadcast_to(piv, (n, 2 * c, LANES)).reshape(C, LANES))
    r4 = row & 3
    up1, up2 = pltpu.roll(b, C - 1, 0), pltpu.roll(b, C - 2, 0)
    dn1, dn2 = pltpu.roll(b, 1, 0), pltpu.roll(b, 2, 0)
    if reverse:
        pivots.append(jnp.where(r4 == 0, up2, jnp.where(r4 == 1, up1, jnp.where(r4 == 2, b, dn1))))
        pivots.append(jnp.where((row & 1) == 0, up1, b))
    else:
        pivots.append(jnp.where(r4 == 0, up1, jnp.where(r4 == 1, b, jnp.where(r4 == 2, dn1, dn2))))
        pivots.append(jnp.where((row & 1) == 0, b, dn1))
    return pivots


HGRN_HEADS_PER_STEP = 2


def _hgrn_level_masks(reverse):
    C = HGRN_CHUNK
    tt = lax.broadcasted_iota(jnp.int32, (C, C), 0)
    ss = lax.broadcasted_iota(jnp.int32, (C, C), 1)
    src, dst = (tt, ss) if reverse else (ss, tt)
    masks = [tt == ss]
    for sh in (5, 4, 3, 2, 1, 0):
        sb, db = src >> sh, dst >> sh
        masks.append(((db & 1) == 1) & (sb == db - 1))
    return jnp.stack(masks).astype(F32)


def _hgrn_kernel(q_ref, z_ref, v_ref, lb_ref, mask_ref, o_ref, st_ref, *, n_chunks, reverse):
    C = HGRN_CHUNK

    @pl.when(pl.program_id(2) == 0)
    def _():
        st_ref[...] = jnp.zeros_like(st_ref)

    row = lax.broadcasted_iota(jnp.int32, (C, LANES), 0)
    last = 0 if reverse else C - 1

    def head_chunk(r0, hh):
        ls = slice(hh * LANES, (hh + 1) * LANES)
        lb = lb_ref[:, ls]
        log_lb = jnp.log(jnp.maximum(lb, LB_FLOOR))
        q = q_ref[pl.ds(r0, C), ls]
        z = z_ref[pl.ds(r0, C), ls]
        v = v_ref[pl.ds(r0, C), ls].astype(BF16)
        log_sig = jnp.minimum(z, 0.0) - jnp.log1p(jnp.exp(-jnp.abs(z)))
        t = jnp.log1p(-lb) + log_sig
        logf = jnp.maximum(log_lb, t) + jnp.log1p(jnp.exp(-jnp.abs(log_lb - t)))
        k = (1.0 - lb) / (1.0 + jnp.exp(z))
        b = logf
        for s in (1, 2, 4, 8, 16, 32):
            if reverse:
                b = b + jnp.where(row < C - s, pltpu.roll(b, C - s, 0), 0.0)
            else:
                b = b + jnp.where(row >= s, pltpu.roll(b, s, 0), 0.0)
        qb, kb = q.astype(BF16), k.astype(BF16)
        a = mask_ref[0] * lax.dot_general(qb, kb, NT_DIMS, preferred_element_type=F32)
        for lvl, piv in enumerate(_hgrn_pivots(b, row, reverse)):
            e = jnp.exp(-jnp.abs(b - piv))
            al = lax.dot_general((q * e).astype(BF16), (k * e).astype(BF16), NT_DIMS,
                                 preferred_element_type=F32)
            a = a + mask_ref[lvl + 1] * al
        st = st_ref[hh]
        b_last = b[last:last + 1, :]
        o = jnp.dot(a.astype(BF16), v, preferred_element_type=F32)
        o = o + lax.dot_general((q * jnp.exp(b)).astype(BF16), st.astype(BF16), NT_DIMS,
                                preferred_element_type=F32)
        o_ref[pl.ds(r0, C), ls] = o
        k_tail = (k * jnp.exp(b_last - b)).astype(BF16)
        st_ref[hh] = st * jnp.exp(b_last) + lax.dot_general(v, k_tail, TN_DIMS,
                                                            preferred_element_type=F32)

    def chunk(i, carry):
        ci = n_chunks - 1 - i if reverse else i
        r0 = pl.multiple_of(ci * C, C)
        for hh in range(HGRN_HEADS_PER_STEP):
            head_chunk(r0, hh)
        return carry

    lax.fori_loop(0, n_chunks, chunk, 0, unroll=4)


def _hgrn_scan(src, off_q, off_z, off_v, lb, B, S, reverse, rb=512):
    rb = min(rb, S)
    nblk = S // rb
    nh = HGRN_HEADS_PER_STEP
    w = nh * LANES
    masks = _hgrn_level_masks(reverse)
    blk_idx = (lambda c: nblk - 1 - c) if reverse else (lambda c: c)
    spec = lambda off: pl.BlockSpec((None, rb, w), lambda b, h, c: (b, blk_idx(c), off // w + h))
    return pl.pallas_call(
        functools.partial(_hgrn_kernel, n_chunks=rb // HGRN_CHUNK, reverse=reverse),
        grid=(B, HGRN_HEADS // nh, nblk),
        in_specs=[spec(off_q), spec(off_z), spec(off_v),
                  pl.BlockSpec((None, 1, w), lambda b, h, c: (h, 0, 0)),
                  pl.BlockSpec(masks.shape, lambda b, h, c: (0, 0, 0))],
        out_specs=pl.BlockSpec((None, rb, w), lambda b, h, c: (b, blk_idx(c), h)),
        out_shape=jax.ShapeDtypeStruct((B, S, HGRN_DK), F32),
        scratch_shapes=[pltpu.VMEM((nh, LANES, LANES), F32)],
        compiler_params=_cparams(("parallel", "parallel", "arbitrary"), 4 * rb * w * 4, 2 * MIB),
        name="hgrn_scan_rev" if reverse else "hgrn_scan")(
            src, src, src, lb.reshape(HGRN_HEADS // nh, 1, w).astype(F32), masks)


def _rope(t, cos, sin, perm):
    rot = jnp.dot(t.astype(BF16), perm, preferred_element_type=F32)
    return t * cos + rot * sin


ATTN_GW = ATTN_G * ATTN_DH
ATTN_PAIRS = ATTN_GW // LANES
ATTN_HEADS_PER_PAIR = LANES // ATTN_DH


def _attn_kernel(*refs, L, dil, radius):
    np_ = ATTN_PAIRS
    q_refs, k_refs, v_refs = refs[0:np_], refs[np_:2 * np_], refs[2 * np_:3 * np_]
    cos_ref, sin_ref = refs[3 * np_], refs[3 * np_ + 1]
    o_refs, l_refs = refs[3 * np_ + 2:4 * np_ + 2], refs[4 * np_ + 2:5 * np_ + 2]
    blk = ATTN_BLOCK
    width = blk + 2 * radius
    m0 = pl.program_id(1) * blk
    mstart = jnp.clip(m0 - radius, 0, L - width)
    pr = lax.broadcasted_iota(jnp.int32, (LANES, LANES), 0)
    pc = lax.broadcasted_iota(jnp.int32, (LANES, LANES), 1)
    half = ROPE_DIM // 2
    in_head = pc & (ATTN_DH - 1)
    src = jnp.where(in_head < half, pc + half, jnp.where(in_head < ROPE_DIM, pc - half, -1))
    perm = jnp.where(pr == src, 1.0, 0.0).astype(BF16)
    row = lax.broadcasted_iota(jnp.int32, (blk, width), 0)
    col = lax.broadcasted_iota(jnp.int32, (blk, width), 1)
    valid = jnp.abs(col - row + (mstart - m0)) <= radius
    lane_head = lax.broadcasted_iota(jnp.int32, (blk, LANES), 1) // ATTN_DH

    def rows_of(first, size, c):
        if dil == 1:
            return pl.ds(first if isinstance(first, int) else pl.multiple_of(first, 8), size)
        return pl.ds(first * dil + c, size, stride=dil)

    def residue(c, carry):
        rq = rows_of(0, blk, c)
        rq_abs = rows_of(m0, blk, c)
        rk = rows_of(mstart, width, c)
        cq, sq = cos_ref[rq_abs, :], sin_ref[rq_abs, :]
        ck, sk = cos_ref[rk, :], sin_ref[rk, :]
        for pair in range(np_):
            qr = _rope(q_refs[pair][rq, :], cq, sq, perm)
            kr = _rope(k_refs[pair][rk, :], ck, sk, perm).astype(BF16)
            vb = v_refs[pair][rk, :].astype(BF16)
            out = jnp.zeros((blk, LANES), F32)
            lse = jnp.zeros((blk, LANES), F32)
            for hd in range(ATTN_HEADS_PER_PAIR):
                mine = lane_head == hd
                s = lax.dot_general(jnp.where(mine, qr, 0.0).astype(BF16), kr, NT_DIMS,
                                    preferred_element_type=F32)
                s = jnp.where(valid, s * (1.0 / math.sqrt(ATTN_DH)), NEG_BIG)
                m = jnp.max(s, axis=-1, keepdims=True)
                p = jnp.exp(s - m)
                den = jnp.sum(p, axis=-1, keepdims=True)
                o = jnp.dot(p.astype(BF16), vb, preferred_element_type=F32)
                out = jnp.where(mine, o / den, out)
                lse = jnp.where(mine, m + jnp.log(den), lse)
            o_refs[pair][rq, :] = out
            l_refs[pair][rq, :] = lse
        return carry

    if dil == 1:
        residue(0, 0)
    else:
        lax.fori_loop(0, dil, residue, 0)


def _band_attention(h3, cos, sin, gi, dil, radius):
    B, S, _ = h3.shape
    L = S // dil
    rows = ATTN_BLOCK * dil
    col = lambda off, pair: (off + gi * ATTN_GW) // LANES + pair
    blocked = lambda off, pair: pl.BlockSpec((None, rows, LANES), lambda b, i: (b, i, col(off, pair)))
    full = lambda off, pair: pl.BlockSpec((None, S, LANES), lambda b, i: (b, 0, col(off, pair)))
    tab = pl.BlockSpec((S, LANES), lambda b, i: (0, 0))
    out = pl.BlockSpec((None, rows, LANES), lambda b, i: (b, i, 0))
    pairs = range(ATTN_PAIRS)
    blk = ATTN_PAIRS * (3 * rows + 2 * S) * LANES * 4 + 2 * S * LANES * 4
    shape = jax.ShapeDtypeStruct((B, S, LANES), F32)
    res = pl.pallas_call(
        functools.partial(_attn_kernel, L=L, dil=dil, radius=radius), grid=(B, L // ATTN_BLOCK),
        in_specs=[blocked(OFF_QA, p) for p in pairs] + [full(OFF_KA, p) for p in pairs]
        + [full(OFF_VA, p) for p in pairs] + [tab, tab],
        out_specs=(out,) * (2 * ATTN_PAIRS), out_shape=(shape,) * (2 * ATTN_PAIRS),
        compiler_params=_cparams(("parallel", "parallel"), blk, 4 * MIB),
        name=f"band_attn_d{dil}")(*([h3] * (3 * ATTN_PAIRS)), cos, sin)
    return res[:ATTN_PAIRS], res[ATTN_PAIRS:]


def _attention_branch(h3):
    S = h3.shape[1]
    pos = jnp.arange(S, dtype=F32)
    inv_freq = ROPE_THETA ** (-jnp.arange(0, ROPE_DIM, 2, dtype=F32) / ROPE_DIM)
    ang = pos[:, None] * inv_freq[None, :]
    cos, sin = jnp.cos(ang), jnp.sin(ang)
    rest = ATTN_DH - ROPE_DIM
    reps = (1, ATTN_HEADS_PER_PAIR)
    cos_f = jnp.tile(jnp.concatenate([cos, cos, jnp.ones((S, rest), F32)], axis=1), reps)
    sin_f = jnp.tile(jnp.concatenate([-sin, sin, jnp.zeros((S, rest), F32)], axis=1), reps)
    return [_band_attention(h3, cos_f, sin_f, gi, dil, window // (2 * dil))
            for gi, (window, dil) in enumerate(ATTN_GROUPS)]


def _sigmoid(x):
    return 1.0 / (1.0 + jnp.exp(-x))


def _prep_kernel(of_ref, ob_ref, g0_ref, g1_ref, g2_ref, ng_ref, *rest):
    n_att = len(ATTN_GROUPS) * ATTN_PAIRS
    a_refs, l_refs, (yh_ref, ya_ref) = rest[:n_att], rest[n_att:2 * n_att], rest[2 * n_att:]
    g_refs = (g0_ref, g1_ref, g2_ref)
    for hd in range(HGRN_HEADS):
        sl = slice(hd * LANES, (hd + 1) * LANES)
        o = of_ref[:, sl] + ob_ref[:, sl]
        o = o * lax.rsqrt(jnp.mean(o * o, axis=-1, keepdims=True) + HGRN_NORM_EPS)
        g = g_refs[hd // 2][:, (hd % 2) * LANES:(hd % 2 + 1) * LANES]
        yh_ref[:, sl] = (o * ng_ref[:, sl] * (g * _sigmoid(g))).astype(BF16)
    for pair in range(ATTN_PAIRS):
        outs = [a_refs[g * ATTN_PAIRS + pair][...] for g in range(len(ATTN_GROUPS))]
        lses = [l_refs[g * ATTN_PAIRS + pair][...] for g in range(len(ATTN_GROUPS))]
        mx = functools.reduce(jnp.maximum, lses)
        es = [jnp.exp(l - mx) for l in lses]
        merged = sum(e * o for e, o in zip(es, outs)) / sum(es)
        ya_ref[:, pair * LANES:(pair + 1) * LANES] = merged.astype(BF16)


def _branch_prep(o_f, o_b, h, norm_g, attn, tm=256):
    T = o_f.shape[0]
    rows = lambda w: pl.BlockSpec((tm, w), lambda i: (i, 0))
    blk = tm * (3 * HGRN_DK + 6 * ATTN_GW) * 4 + tm * (HGRN_DK + ATTN_GW) * 2
    gw = 2 * LANES
    gate = lambda j: pl.BlockSpec((tm, gw), lambda i: (i, OFF_G // gw + j))
    outs = [o for os_, _ in attn for o in os_]
    lses = [l for _, ls_ in attn for l in ls_]
    return pl.pallas_call(
        _prep_kernel, grid=(T // tm,),
        in_specs=[rows(HGRN_DK), rows(HGRN_DK), gate(0), gate(1), gate(2),
                  pl.BlockSpec((1, HGRN_DK), lambda i: (0, 0))]
        + [rows(LANES)] * (len(outs) + len(lses)),
        out_specs=(rows(HGRN_DK), rows(ATTN_GW)),
        out_shape=(jax.ShapeDtypeStruct((T, HGRN_DK), BF16),
                   jax.ShapeDtypeStruct((T, ATTN_GW), BF16)),
        compiler_params=_cparams(("parallel",), blk, 2 * MIB), name="branch_prep")(
            o_f, o_b, h, h, h, norm_g.reshape(1, HGRN_DK).astype(F32), *outs, *lses)


def _merge_kernel(yf_ref, yh_ref, ya_ref, wf_ref, wh_ref, wa_ref, g0_ref, g1_ref, g2_ref, o_ref):
    dot = lambda y, w: jnp.dot(y[...], w[...], preferred_element_type=F32)
    gate = lambda g_ref: _sigmoid(g_ref[...].astype(F32))
    acc = gate(g0_ref) * dot(yf_ref, wf_ref)
    acc = acc + gate(g1_ref) * dot(yh_ref, wh_ref)
    acc = acc + gate(g2_ref) * dot(ya_ref, wa_ref)
    o_ref[...] = acc.astype(o_ref.dtype)


def _gated_merge(y_f, y_h, y_a, w_f, w_h, w_a, gates, D, tm=1024, tn=512):
    T = y_f.shape[0]
    tm = min(tm, T)
    kf, kh, ka = y_f.shape[1], y_h.shape[1], y_a.shape[1]
    rows = lambda w: pl.BlockSpec((tm, w), lambda j, i: (i, 0))
    cols = lambda kk: pl.BlockSpec((kk, tn), lambda j, i: (0, j))
    gate = lambda br: pl.BlockSpec((tm, tn), lambda j, i: (i, br * D // tn + j))
    blk = tm * (kf + kh + ka) * 2 + (kf + kh + ka) * tn * 2 + 3 * tm * tn * 2 + tm * tn * 2
    return pl.pallas_call(
        _merge_kernel, grid=(D // tn, T // tm),
        in_specs=[rows(kf), rows(kh), rows(ka), cols(kf), cols(kh), cols(ka),
                  gate(0), gate(1), gate(2)],
        out_specs=pl.BlockSpec((tm, tn), lambda j, i: (i, j)),
        out_shape=jax.ShapeDtypeStruct((T, D), BF16),
        compiler_params=_cparams(("parallel", "parallel"), blk, 4 * tm * tn * 4),
        name="gated_merge")(y_f, y_h, y_a, w_f, w_h, w_a, gates, gates, gates)


def _top_rows(s, n, with_rank=False):
    rowi = lax.broadcasted_iota(jnp.int32, (n, s.shape[1]), 0)
    top = jnp.full((n, s.shape[1]), NEG_BIG, F32)
    rank = jnp.full(s.shape, float(n), F32)
    for r in range(n):
        m = jnp.max(s, axis=0, keepdims=True)
        top = jnp.where(rowi == r, m, top)
        hit = s == m
        if with_rank:
            rank = jnp.where(hit, float(r), rank)
        s = jnp.where(hit, NEG_BIG, s)
    return top, jnp.max(s, axis=0, keepdims=True), rank


def _peer_prep_kernel(q_ref, keys_ref, n_ref, a_ref, rk_ref, bp_ref):
    K = PEER_TOPK
    nk = PEER_N_KEYS
    for hd in range(PEER_HEADS):
        sc = []
        for p in range(2):
            qb = q_ref[:, (2 * hd + p) * nk:(2 * hd + p + 1) * nk]
            sc.append(lax.dot_general(keys_ref[p], qb, NT_DIMS, precision=lax.Precision.HIGHEST,
                                      preferred_element_type=F32))
        s1, s2 = sc
        r1, x1, _ = _top_rows(s1, K)
        r2, x2, rank2 = _top_rows(s2, K, with_rank=True)
        cand = jnp.concatenate([r1[0:1, :] + r2] + [r1[a:a + 1, :] + r2[0:8, :] for a in range(1, K)],
                               axis=0)
        ctop, c17, _ = _top_rows(cand, K)
        c16 = ctop[K - 1:K, :]
        c17 = jnp.maximum(c17, jnp.maximum(x1 + r2[0:1, :], r1[0:1, :] + x2))
        tau = 0.5 * (c16 + c17)
        m1, m2 = r1[0:1, :], r2[0:1, :]
        z = jnp.zeros_like(tau)
        for a in range(K):
            ra = r1[a:a + 1, :]
            z = z + jnp.exp(ra - m1) * jnp.sum(
                jnp.where(r2 >= tau - ra, jnp.exp(r2 - m2), 0.0), axis=0, keepdims=True)
        bound = tau - s1
        count = jnp.zeros_like(s1)
        for b in range(K):
            count = count + jnp.where(r2[b:b + 1, :] >= bound, 1.0, 0.0)
        n_ref[hd] = count
        a_ref[hd] = 0.5 * jnp.exp(s1 - m1)
        rk_ref[hd] = rank2.astype(BF16)
        bp_ref[hd] = (jnp.exp(s2 - m2) / z).astype(BF16)


def _peer_prep(qp, sub_keys, tb=256):
    T, W = qp.shape
    tb = min(tb, T)
    nk = PEER_N_KEYS
    out = pl.BlockSpec((PEER_HEADS, nk, tb), lambda i: (0, 0, i))
    blk = tb * W * 4 + 2 * nk * nk * 4 + PEER_HEADS * nk * tb * (4 + 4 + 2 + 2)
    shape = lambda dt: jax.ShapeDtypeStruct((PEER_HEADS, nk, T), dt)
    return pl.pallas_call(
        _peer_prep_kernel, grid=(T // tb,),
        in_specs=[pl.BlockSpec((tb, W), lambda i: (i, 0)),
                  pl.BlockSpec((2, nk, sub_keys.shape[2]), lambda i: (0, 0, 0))],
        out_specs=(out,) * 4,
        out_shape=(shape(F32), shape(F32), shape(BF16), shape(BF16)),
        compiler_params=_cparams(("parallel",), blk, 4 * MIB), name="peer_prep")(qp, sub_keys)


def _peer_main_kernel(x_ref, u_ref, n_ref, a_ref, rk_ref, bp_ref, p_ref, *, n_part, n_sub):
    nk = PEER_N_KEYS
    e = pl.program_id(1)
    rows = n_sub * nk
    tb = p_ref.shape[1]

    for part in range(n_part):
        lo = part * rows
        hid = lax.dot_general(u_ref[lo:lo + rows, :], x_ref[...], NT_DIMS,
                              preferred_element_type=F32)
        for j in range(n_sub):
            i1 = (e * n_part + part) * n_sub + j
            ns = [n_ref[hd, pl.ds(i1, 1), :] for hd in range(PEER_HEADS)]
            avs = [a_ref[hd, pl.ds(i1, 1), :] for hd in range(PEER_HEADS)]
            for c in range(tb // LANES):
                ls = slice(c * LANES, (c + 1) * LANES)
                spread = lambda r: jnp.broadcast_to(r[:, ls], (BF16_ROWS, LANES)).astype(BF16)[None]
                tiles = lambda ref, hd: ref[hd, :, ls].reshape(nk // BF16_ROWS, BF16_ROWS, LANES)
                w = jnp.zeros((nk // BF16_ROWS, BF16_ROWS, LANES), BF16)
                for hd in range(PEER_HEADS):
                    w = w + jnp.where(tiles(rk_ref, hd) < spread(ns[hd]), tiles(bp_ref, hd),
                                      jnp.zeros((), BF16)) * spread(avs[hd])
                hj = hid[j * nk:(j + 1) * nk, ls]
                act = hj * (1.0 + lax.erf(hj * (1.0 / math.sqrt(2.0))))
                p_ref[lo + j * nk:lo + (j + 1) * nk, ls] = act.astype(BF16) * w.reshape(nk, LANES)


def _peer_out_kernel(pt_ref, v_ref, o_ref):
    @pl.when(pl.program_id(1) == 0)
    def _():
        o_ref[...] = jnp.zeros_like(o_ref)

    o_ref[...] += lax.dot_general(pt_ref[...], v_ref[...], TN_DIMS, preferred_element_type=F32)


def _peer_main(x_bf, u, v, n, a, rk, bp, tb=512, eb=1024, n_part=1, tm=1024, tk=2048):
    T, D = x_bf.shape
    E = u.shape[0]
    tb, tm = min(tb, T), min(tm, T)
    nk = PEER_N_KEYS
    tok = pl.BlockSpec((PEER_HEADS, nk, tb), lambda t, e: (0, 0, t))
    blk = tb * D * 2 + eb * D * 2 + PEER_HEADS * nk * tb * (4 + 4 + 2 + 2) + eb * tb * 2
    pt = pl.pallas_call(
        functools.partial(_peer_main_kernel, n_part=n_part, n_sub=eb // (n_part * nk)),
        grid=(T // tb, E // eb),
        in_specs=[pl.BlockSpec((tb, D), lambda t, e: (t, 0)),
                  pl.BlockSpec((eb, D), lambda t, e: (e, 0)),
                  tok, tok, tok, tok],
        out_specs=pl.BlockSpec((eb, tb), lambda t, e: (e, t)),
        out_shape=jax.ShapeDtypeStruct((E, T), BF16),
        compiler_params=_cparams(("parallel", "parallel"), blk, 2 * eb * tb * 4),
        name="peer_main")(x_bf, u, n, a, rk, bp)
    blk = tk * tm * 2 + tk * D * 2 + tm * D * 4
    return pl.pallas_call(
        _peer_out_kernel, grid=(T // tm, E // tk),
        in_specs=[pl.BlockSpec((tk, tm), lambda t, k: (k, t)),
                  pl.BlockSpec((tk, D), lambda t, k: (k, 0))],
        out_specs=pl.BlockSpec((tm, D), lambda t, k: (t, 0)),
        out_shape=jax.ShapeDtypeStruct((T, D), F32),
        compiler_params=_cparams(("parallel", "arbitrary"), blk, tm * D * 4),
        name="peer_out")(pt, v)


def _peer_ffn(x_bf, w_query, sub_keys, u, v, layer):
    qp = _matmul(x_bf, w_query, layer, F32, 2048, 512, "peer_query")
    n, a, rk, bp = _peer_prep(qp, sub_keys.astype(F32))
    return _peer_main(x_bf, _cast_layer(u, layer), _cast_layer(v, layer), n, a, rk, bp)


def _hybrid_mixer(x_bf, w_in, w_br_f, w_br_h, w_br_a, w_out, lb_f, lb_b, norm_g, B, S, layer):
    T, D = x_bf.shape
    h = _matmul(x_bf, w_in, layer, F32, 2048, 512, "in_proj", cols=(0, OFF_GATE))
    gates = _matmul(x_bf, w_in, layer, BF16, 2048, 512, "in_proj_gates",
                    cols=(OFF_GATE, N_BRANCHES * D))
    h3 = h.reshape(B, S, h.shape[1])
    y_f = _fourier_branch(h, B, S)
    o_f = _hgrn_scan(h3, OFF_Q, OFF_FF, OFF_I, lb_f, B, S, reverse=False)
    o_b = _hgrn_scan(h3, OFF_Q, OFF_FB, OFF_I, lb_b, B, S, reverse=True)
    flat = lambda ts: tuple(t.reshape(T, LANES) for t in ts)
    attn = [(flat(os_), flat(ls_)) for os_, ls_ in _attention_branch(h3)]
    y_h, y_a = _branch_prep(o_f.reshape(T, HGRN_DK), o_b.reshape(T, HGRN_DK), h, norm_g, attn)
    merged = _gated_merge(y_f, y_h, y_a, w_br_f.astype(BF16), w_br_h.astype(BF16),
                          w_br_a.astype(BF16), gates, D)
    return _matmul(merged, w_out, layer, F32, 2048, 512, "out_proj")


def kernel(x, emb_ln_g, emb_ln_b, w_in, w_br_fourier, w_br_hgrn, w_br_attn, w_out,
           hgrn_lb_logits, hgrn_norm_g, ln_mix_g, ln_mix_b, peer_w_query, peer_sub_keys,
           peer_u, peer_v, ln_ffn_g, ln_ffn_b):
    B, S, D = x.shape
    T = B * S
    depth = w_in.shape[0]
    alpha = (2.0 * depth) ** 0.25
    lb_p = jax.nn.softmax(hgrn_lb_logits.astype(F32), axis=1)
    lb = jnp.cumsum(lb_p, axis=1) - lb_p[:, :1]
    xf, xb = _layer_norm(x.reshape(T, D), emb_ln_g, emb_ln_b)
    for l in range(depth):
        mix = _hybrid_mixer(xb, w_in, w_br_fourier[l], w_br_hgrn[l], w_br_attn[l], w_out,
                            lb[0, l], lb[1, l], hgrn_norm_g[l], B, S, l)
        xf, xb = _layer_norm(xf, ln_mix_g[l], ln_mix_b[l], res=mix, alpha=alpha)
        ffn = _peer_ffn(xb, peer_w_query, peer_sub_keys[l], peer_u, peer_v, l)
        xf, xb = _layer_norm(xf, ln_ffn_g[l], ln_ffn_b[l], res=ffn, alpha=alpha)
    return xf.reshape(B, S, D)
```
